```python
import math
import jax, jax.numpy as jnp
from jax import lax
import numpy as np

D_MODEL = 2048
BATCH = 8
SEQ = 2048
DEPTH = 2
DEC_BATCH = 32
DEC_SEQ = 8
PAST_LEN = 8192
PAGE_SIZE = 128

N_EVEN = (DEPTH + 1) // 2
N_ODD = DEPTH // 2
EPS = 1e-6
NEG = -1e30
CONV_CH = D_MODEL
CONV_WIDTH = 31
SSD_HEAD_DIM = 64
SSD_HEADS = D_MODEL // SSD_HEAD_DIM
SSD_INNER = SSD_HEADS * SSD_HEAD_DIM
SSD_GROUPS = 4
SSD_STATE = 128
SSD_CONV = 4
SSD_CHUNK = 128
SSD_XBC = SSD_INNER + 2 * SSD_GROUPS * SSD_STATE
EVEN_IN = 3 * CONV_CH + SSD_INNER + SSD_XBC + SSD_HEADS
EVEN_MIX = CONV_CH + SSD_INNER
ATT_QK_DIM = 64
ATT_V_DIM = 2 * ATT_QK_DIM
ATT_HEADS = D_MODEL // ATT_V_DIM
ATT_QK = ATT_HEADS * 2 * ATT_QK_DIM
ATT_WIDTH = ATT_HEADS * ATT_V_DIM
ODD_IN = 2 * ATT_QK + 2 * ATT_WIDTH
Q_BLOCK = 128

kernel_name = 'hybrid_conv_ssd_diffattn_step'


def rms_norm(x, g):
    x32 = x.astype(jnp.float32)
    y = x32 * lax.rsqrt(jnp.mean(x32 * x32, axis=-1, keepdims=True) + EPS)
    return (y * g.astype(jnp.float32)).astype(x.dtype)


def layer_norm(x, g, b):
    x32 = x.astype(jnp.float32)
    xc = x32 - jnp.mean(x32, axis=-1, keepdims=True)
    y = xc * lax.rsqrt(jnp.mean(xc * xc, axis=-1, keepdims=True) + EPS)
    return (y * g.astype(jnp.float32) + b.astype(jnp.float32)).astype(x.dtype)


def causal_dwconv(x, prefix, w, b):
    xp = jnp.concatenate([prefix.astype(x.dtype), x], axis=1)
    y = lax.conv_general_dilated(xp, w[:, None, :].astype(x.dtype), (1,), 'VALID',
                                 dimension_numbers=('NWC', 'WIO', 'NWC'),
                                 feature_group_count=x.shape[-1])
    return y + b.astype(x.dtype), xp[:, xp.shape[1] - (w.shape[0] - 1):]


def ssd_chunked(xh, dt, a, bm, cm, h0):
    f32 = jnp.float32
    nb, L = xh.shape[0], xh.shape[1]
    q = SSD_CHUNK if L % SSD_CHUNK == 0 else L
    c = L // q
    g, e = SSD_GROUPS, SSD_HEADS // SSD_GROUPS
    x = xh.astype(f32).reshape(nb, c, q, g, e, SSD_HEAD_DIM)
    dtc = dt.reshape(nb, c, q, g, e)
    B = bm.astype(f32).reshape(nb, c, q, g, SSD_STATE)
    C = cm.astype(f32).reshape(nb, c, q, g, SSD_STATE)
    acs = jnp.cumsum(dtc * a.reshape(g, e), axis=2)
    at = jnp.moveaxis(acs, 2, -1)
    causal = jnp.tril(jnp.ones((q, q), dtype=bool))
    lmat = jnp.exp(jnp.where(causal, at[..., :, None] - at[..., None, :], -jnp.inf))
    cb = jnp.einsum('bclgn,bcsgn->bcgls', C, B)
    y_diag = jnp.einsum('bcgls,bcgels,bcsge,bcsgep->bclgep', cb, lmat, dtc, x)
    decay_in = jnp.exp(at[..., -1:] - at)
    chunk_states = jnp.einsum('bclgn,bcgel,bclge,bclgep->bcgepn', B, decay_in, dtc, x)
    chunk_decay = jnp.exp(at[..., -1])

    def step(h, inp):
        s_c, d_c = inp
        return h * d_c[..., None, None] + s_c, h

    h_init = h0.astype(f32).reshape(nb, g, e, SSD_HEAD_DIM, SSD_STATE)
    h_last, h_in = lax.scan(step, h_init, (jnp.moveaxis(chunk_states, 1, 0), jnp.moveaxis(chunk_decay, 1, 0)))
    h_in = jnp.moveaxis(h_in, 0, 1)
    y_off = jnp.einsum('bclgn,bcgepn,bcgel->bclgep', C, h_in, jnp.exp(at))
    y = (y_diag + y_off).reshape(nb, L, SSD_HEADS, SSD_HEAD_DIM)
    return y, h_last.reshape(nb, SSD_HEADS, SSD_HEAD_DIM, SSD_STATE)


def even_layer(x, conv_prev, ssd_conv_prev, ssd_prev, norm_g, w_in, conv_w, conv_b, ln_g, ln_b,
               ssd_conv_w, ssd_conv_b, dt_bias, a_log, d_skip, ssd_norm_g, w_out):
    f32 = jnp.float32
    nb, L = x.shape[0], x.shape[1]
    h = rms_norm(x, norm_g)
    proj = jnp.einsum('bld,de->ble', h, w_in)
    cuts = [CONV_CH, 2 * CONV_CH, 3 * CONV_CH, 3 * CONV_CH + SSD_INNER, 3 * CONV_CH + SSD_INNER + SSD_XBC]
    a_val, a_glu, a_gate, z, xbc, dt_raw = jnp.split(proj, cuts, axis=-1)
    u = a_val * jax.nn.sigmoid(a_glu)
    u, conv_new = causal_dwconv(u, conv_prev, conv_w, conv_b)
    ya = jax.nn.silu(layer_norm(u, ln_g, ln_b)) * jax.nn.silu(a_gate)
    xbc, ssd_conv_new = causal_dwconv(xbc, ssd_conv_prev, ssd_conv_w, ssd_conv_b)
    xbc = jax.nn.silu(xbc)
    xs, bm, cm = jnp.split(xbc, [SSD_INNER, SSD_INNER + SSD_GROUPS * SSD_STATE], axis=-1)
    xs = xs.reshape(nb, L, SSD_HEADS, SSD_HEAD_DIM)
    dt = jax.nn.softplus(dt_raw.astype(f32) + dt_bias.astype(f32))
    a = -jnp.exp(a_log.astype(f32))
    y, ssd_new = ssd_chunked(xs, dt, a, bm.reshape(nb, L, SSD_GROUPS, SSD_STATE),
                             cm.reshape(nb, L, SSD_GROUPS, SSD_STATE), ssd_prev)
    y = y + d_skip.astype(f32)[:, None] * xs.astype(f32)
    y = (y.reshape(nb, L, SSD_INNER) * jax.nn.silu(z.astype(f32))).reshape(nb, L, SSD_GROUPS, SSD_INNER // SSD_GROUPS)
    y = y * lax.rsqrt(jnp.mean(y * y, axis=-1, keepdims=True) + EPS)
    yb = (y.reshape(nb, L, SSD_INNER) * ssd_norm_g.astype(f32)).astype(x.dtype)
    out = jnp.einsum('ble,ed->bld', jnp.concatenate([ya, yb], axis=-1), w_out)
    return x + out, conv_new, ssd_conv_new, ssd_new.astype(ssd_prev.dtype)


def odd_project(x, norm_g, w_in, q_g, k_g):
    nb, L = x.shape[0], x.shape[1]
    h = rms_norm(x, norm_g)
    proj = jnp.einsum('bld,de->ble', h, w_in)
    q, k, v, gate = jnp.split(proj, [ATT_QK, 2 * ATT_QK, 2 * ATT_QK + ATT_WIDTH], axis=-1)
    q = rms_norm(q.reshape(nb, L, ATT_HEADS, 2, ATT_QK_DIM), q_g)
    k = rms_norm(k.reshape(nb, L, ATT_HEADS, 2, ATT_QK_DIM), k_g)
    v = v.reshape(nb, L, ATT_HEADS, ATT_V_DIM)
    return q, k, v, gate


def attn_partial(q, k, v, mask):
    s = jnp.einsum('bqhmd,bkhmd->bmhqk', q, k)
    if mask is not None:
        s = jnp.where(mask, s, NEG)
    m = jnp.max(s, axis=-1)
    p = jnp.exp(s - m[..., None])
    return m, jnp.sum(p, axis=-1), jnp.einsum('bmhqk,bkhd->bmhqd', p, v)


def merge(p1, p2):
    m1, l1, a1 = p1
    m2, l2, a2 = p2
    m = jnp.maximum(m1, m2)
    s1 = jnp.exp(m1 - m)
    s2 = jnp.exp(m2 - m)
    return m, l1 * s1 + l2 * s2, a1 * s1[..., None] + a2 * s2[..., None]


def diff_combine(part, lam):
    m, l, acc = part
    o = acc / l[..., None]
    return jnp.moveaxis(o[:, 0] - lam * o[:, 1], 2, 1)


def prompt_attention(q, k, v, lam):
    f32 = jnp.float32
    nb, L = q.shape[0], q.shape[1]
    n_blk = L // Q_BLOCK
    qf = q.astype(f32) * (ATT_QK_DIM ** -0.5)
    kf = k.astype(f32)
    vf = v.astype(f32)
    qb = jnp.moveaxis(qf.reshape(nb, n_blk, Q_BLOCK, ATT_HEADS, 2, ATT_QK_DIM), 1, 0)
    kpos = jnp.arange(L)

    def block(inp):
        i, qi = inp
        qpos = i * Q_BLOCK + jnp.arange(Q_BLOCK)
        mask = kpos[None, :] <= qpos[:, None]
        return diff_combine(attn_partial(qi, kf, vf, mask), lam)

    o = lax.map(block, (jnp.arange(n_blk), qb))
    return jnp.moveaxis(o, 0, 1).reshape(nb, L, ATT_HEADS, ATT_V_DIM)


def sample_attention(q, k, v, cache_k, cache_v, page_table, j, lam):
    f32 = jnp.float32
    nb, L = q.shape[0], q.shape[1]
    qf = q.astype(f32) * (ATT_QK_DIM ** -0.5)
    init = (jnp.full((nb, 2, ATT_HEADS, L), NEG, f32),
            jnp.zeros((nb, 2, ATT_HEADS, L), f32),
            jnp.zeros((nb, 2, ATT_HEADS, L, ATT_V_DIM), f32))

    def page_step(carry, pages):
        kp = cache_k[j, pages].astype(f32)
        vp = cache_v[j, pages].astype(f32)
        return merge(carry, attn_partial(qf, kp, vp, None)), None

    part, _ = lax.scan(page_step, init, page_table.T)
    mask = jnp.tril(jnp.ones((L, L), dtype=bool))
    part = merge(part, attn_partial(qf, k.astype(f32), v.astype(f32), mask))
    return diff_combine(part, lam)


def odd_output(x, o, gate, lam_init, sub_g, w_out):
    nb, L = x.shape[0], x.shape[1]
    o = rms_norm(o, sub_g) * (1.0 - lam_init)
    o = o.reshape(nb, L, ATT_WIDTH) * jax.nn.silu(gate.astype(jnp.float32))
    return x + jnp.einsum('ble,ed->bld', o.astype(x.dtype), w_out)


def setup_inputs(seed: int = 0) -> dict:
    key = jax.random.key(seed)
    ks = iter(jax.random.split(key, 48))
    f32 = jnp.float32

    def nrm(shape, scale=1.0):
        return scale * jax.random.normal(next(ks), shape, f32)

    def gain(shape):
        return 1.0 + nrm(shape, 0.02)

    n_pages = PAST_LEN // PAGE_SIZE
    n_used = DEC_BATCH * n_pages
    n_pool = n_used + max(1, n_used // 4)
    x_prompt = nrm((BATCH, SEQ, D_MODEL))
    x_sample = nrm((DEC_BATCH, DEC_SEQ, D_MODEL))
    state_conv_a = nrm((N_EVEN, DEC_BATCH, CONV_WIDTH - 1, CONV_CH), 0.5)
    state_ssd_conv = nrm((N_EVEN, DEC_BATCH, SSD_CONV - 1, SSD_XBC))
    state_ssd = nrm((N_EVEN, DEC_BATCH, SSD_HEADS, SSD_HEAD_DIM, SSD_STATE), 0.5)
    cache_k = nrm((N_ODD, n_pool, PAGE_SIZE, ATT_HEADS, 2, ATT_QK_DIM))
    cache_v = nrm((N_ODD, n_pool, PAGE_SIZE, ATT_HEADS, ATT_V_DIM))
    page_table = jax.random.permutation(next(ks), n_pool)[:n_used].reshape(DEC_BATCH, n_pages).astype(jnp.int32)
    dt0 = jnp.exp(jax.random.uniform(next(ks), (N_EVEN, SSD_HEADS), f32, math.log(1e-3), math.log(1e-1)))
    ssd_dt_bias = dt0 + jnp.log(-jnp.expm1(-dt0))
    ssd_a_log = jnp.log(jax.random.uniform(next(ks), (N_EVEN, SSD_HEADS), f32, 1.0, 16.0))
    return {
        'x_prompt': x_prompt,
        'x_sample': x_sample,
        'state_conv_a': state_conv_a,
        'state_ssd_conv': state_ssd_conv,
        'state_ssd': state_ssd,
        'cache_k': cache_k,
        'cache_v': cache_v,
        'page_table': page_table,
        'norm_even': gain((N_EVEN, D_MODEL)),
        'w_in_even': nrm((N_EVEN, D_MODEL, EVEN_IN), D_MODEL ** -0.5),
        'conv_a_w': nrm((N_EVEN, CONV_WIDTH, CONV_CH), CONV_WIDTH ** -0.5),
        'conv_a_b': nrm((N_EVEN, CONV_CH), 0.01),
        'conv_a_ln_g': gain((N_EVEN, CONV_CH)),
        'conv_a_ln_b': nrm((N_EVEN, CONV_CH), 0.01),
        'ssd_conv_w': nrm((N_EVEN, SSD_CONV, SSD_XBC), SSD_CONV ** -0.5),
        'ssd_conv_b': nrm((N_EVEN, SSD_XBC), 0.01),
        'ssd_dt_bias': ssd_dt_bias,
        'ssd_a_log': ssd_a_log,
        'ssd_d': gain((N_EVEN, SSD_HEADS)),
        'ssd_norm_g': gain((N_EVEN, SSD_INNER)),
        'w_out_even': nrm((N_EVEN, EVEN_MIX, D_MODEL), EVEN_MIX ** -0.5),
        'norm_odd': gain((N_ODD, D_MODEL)),
        'w_in_odd': nrm((N_ODD, D_MODEL, ODD_IN), D_MODEL ** -0.5),
        'q_norm_g': gain((N_ODD, ATT_QK_DIM)),
        'k_norm_g': gain((N_ODD, ATT_QK_DIM)),
        'lam_q1': nrm((N_ODD, ATT_QK_DIM), 0.1),
        'lam_k1': nrm((N_ODD, ATT_QK_DIM), 0.1),
        'lam_q2': nrm((N_ODD, ATT_QK_DIM), 0.1),
        'lam_k2': nrm((N_ODD, ATT_QK_DIM), 0.1),
        'subln_g': gain((N_ODD, ATT_V_DIM)),
        'w_out_odd': nrm((N_ODD, ATT_WIDTH, D_MODEL), ATT_WIDTH ** -0.5),
    }


def reference(x_prompt, x_sample, state_conv_a, state_ssd_conv, state_ssd, cache_k, cache_v, page_table,
              norm_even, w_in_even, conv_a_w, conv_a_b, conv_a_ln_g, conv_a_ln_b,
              ssd_conv_w, ssd_conv_b, ssd_dt_bias, ssd_a_log, ssd_d, ssd_norm_g, w_out_even,
              norm_odd, w_in_odd, q_norm_g, k_norm_g, lam_q1, lam_k1, lam_q2, lam_k2, subln_g, w_out_odd):
    f32 = jnp.float32
    yp, ys = x_prompt, x_sample
    bp = x_prompt.shape[0]
    ca_p, ca_s, sc_p, sc_s, ss_p, ss_s = [], [], [], [], [], []
    k_p, k_s, v_p, v_s = [], [], [], []
    for layer in range(DEPTH):
        j = layer // 2
        if layer % 2 == 0:
            w = (norm_even[j], w_in_even[j], conv_a_w[j], conv_a_b[j], conv_a_ln_g[j], conv_a_ln_b[j],
                 ssd_conv_w[j], ssd_conv_b[j], ssd_dt_bias[j], ssd_a_log[j], ssd_d[j], ssd_norm_g[j], w_out_even[j])
            zc = jnp.zeros((bp, CONV_WIDTH - 1, CONV_CH), yp.dtype)
            zx = jnp.zeros((bp, SSD_CONV - 1, SSD_XBC), yp.dtype)
            zs = jnp.zeros((bp, SSD_HEADS, SSD_HEAD_DIM, SSD_STATE), state_ssd.dtype)
            yp, c1, c2, c3 = even_layer(yp, zc, zx, zs, *w)
            ys, d1, d2, d3 = even_layer(ys, state_conv_a[j], state_ssd_conv[j], state_ssd[j], *w)
            ca_p.append(c1.astype(state_conv_a.dtype))
            sc_p.append(c2.astype(state_ssd_conv.dtype))
            ss_p.append(c3)
            ca_s.append(d1.astype(state_conv_a.dtype))
            sc_s.append(d2.astype(state_ssd_conv.dtype))
            ss_s.append(d3)
        else:
            lam_init = 0.8 - 0.6 * math.exp(-0.3 * layer)
            lam = (jnp.exp(jnp.sum(lam_q1[j].astype(f32) * lam_k1[j].astype(f32)))
                   - jnp.exp(jnp.sum(lam_q2[j].astype(f32) * lam_k2[j].astype(f32))) + lam_init)
            pw = (norm_odd[j], w_in_odd[j], q_norm_g[j], k_norm_g[j])
            q, k, v, gate = odd_project(yp, *pw)
            o = prompt_attention(q, k, v, lam)
            yp = odd_output(yp, o, gate, lam_init, subln_g[j], w_out_odd[j])
            k_p.append(k.astype(cache_k.dtype))
            v_p.append(v.astype(cache_v.dtype))
            q, k, v, gate = odd_project(ys, *pw)
            o = sample_attention(q, k, v, cache_k, cache_v, page_table, j, lam)
            ys = odd_output(ys, o, gate, lam_init, subln_g[j], w_out_odd[j])
            k_s.append(k.astype(cache_k.dtype))
            v_s.append(v.astype(cache_v.dtype))
    return (yp, ys, jnp.stack(ca_p), jnp.stack(ca_s), jnp.stack(sc_p), jnp.stack(sc_s),
            jnp.stack(ss_p), jnp.stack(ss_s), jnp.stack(k_p), jnp.stack(k_s), jnp.stack(v_p), jnp.stack(v_s))
```

```python
import functools
import math

import jax
import jax.numpy as jnp
from jax import lax
from jax.experimental import pallas as pl
from jax.experimental.pallas import tpu as pltpu

F32 = jnp.float32
BF16 = jnp.bfloat16

EPS = 1e-6
NEG = -1e30
LANES = 128
CONV_WIDTH = 31
CONV_PAD = 32
SSD_HEAD_DIM = 64
SSD_GROUPS = 4
SSD_STATE = 128
SSD_CONV = 4
SSD_PAD = 8
SSD_CHUNK = 128
ATT_QK_DIM = 64
ATT_V_DIM = 128
PAGE = 128
VMEM_LIMIT = 56 * 1024 * 1024


def _cparams(*sem):
    return pltpu.CompilerParams(dimension_semantics=sem, vmem_limit_bytes=VMEM_LIMIT)


def _silu(x):
    return x * (1.0 / (1.0 + jnp.exp(-x)))


def _sigmoid(x):
    return 1.0 / (1.0 + jnp.exp(-x))


def _split2(x):
    hi = x.astype(BF16)
    lo = (x - hi.astype(F32)).astype(BF16)
    return hi, lo


def _split3(x):
    hi = x.astype(BF16)
    r = x - hi.astype(F32)
    mid = r.astype(BF16)
    lo = (r - mid.astype(F32)).astype(BF16)
    return hi, mid, lo


def _dot(a, b):
    return jnp.dot(a, b, preferred_element_type=F32)


def _dot_nt(a, b):
    return lax.dot_general(a, b, (((1,), (1,)), ((), ())), preferred_element_type=F32)


def _dot_tn(a, b):
    return lax.dot_general(a, b, (((0,), (0,)), ((), ())), preferred_element_type=F32)


def _norm_matmul_kernel(x_ref, g_ref, w_ref, *rest, has_aux):
    if has_aux:
        wa_ref, o_ref, oa_ref, hn_ref = rest
    else:
        o_ref, hn_ref = rest

    @pl.when(pl.program_id(1) == 0)
    def _():
        x = x_ref[...]
        ms = jnp.mean(x * x, axis=-1, keepdims=True)
        hn_ref[...] = (x * lax.rsqrt(ms + EPS) * g_ref[...]).astype(BF16)
        if has_aux:
            oa_ref[...] = _dot(hn_ref[...], wa_ref[...])

    o_ref[...] = _dot(hn_ref[...], w_ref[...])


def _norm_matmul(x, g, w, w_aux=None, *, tm, tn):
    m, k = x.shape
    n = w.shape[1]
    assert m % tm == 0 and n % tn == 0
    has_aux = w_aux is not None
    in_specs = [pl.BlockSpec((tm, k), lambda i, j: (i, 0)),
                pl.BlockSpec((1, k), lambda i, j: (0, 0)),
                pl.BlockSpec((k, tn), lambda i, j: (0, j))]
    out_specs = [pl.BlockSpec((tm, tn), lambda i, j: (i, j))]
    out_shape = [jax.ShapeDtypeStruct((m, n), F32)]
    args = [x, g.reshape(1, k), w]
    if has_aux:
        na = w_aux.shape[1]
        in_specs.append(pl.BlockSpec((k, na), lambda i, j: (0, 0)))
        out_specs.append(pl.BlockSpec((tm, na), lambda i, j: (i, 0)))
        out_shape.append(jax.ShapeDtypeStruct((m, na), F32))
        args.append(w_aux)
    outs = pl.pallas_call(
        functools.partial(_norm_matmul_kernel, has_aux=has_aux),
        grid=(m // tm, n // tn),
        in_specs=in_specs, out_specs=out_specs, out_shape=out_shape,
        scratch_shapes=[pltpu.VMEM((tm, k), BF16)],
        compiler_params=_cparams("parallel", "arbitrary"),
        name="norm_matmul",
    )(*args)
    return outs if has_aux else outs[0]


def _matmul_res_kernel(*refs, n_parts):
    a_refs = refs[:n_parts]
    w_refs = refs[n_parts:2 * n_parts]
    res_ref, o_ref = refs[2 * n_parts], refs[2 * n_parts + 1]
    acc = res_ref[...]
    for a_ref, w_ref in zip(a_refs, w_refs):
        acc = acc + _dot(a_ref[...].astype(BF16), w_ref[...])
    o_ref[...] = acc


def _matmul_res(a_parts, w_parts, res, *, tm, tn):
    m, n = res.shape
    assert m % tm == 0 and n % tn == 0
    n_parts = len(a_parts)
    in_specs = ([pl.BlockSpec((tm, a.shape[1]), lambda i, j: (i, 0)) for a in a_parts]
                + [pl.BlockSpec((w.shape[0], tn), lambda i, j: (0, j)) for w in w_parts]
                + [pl.BlockSpec((tm, tn), lambda i, j: (i, j))])
    return pl.pallas_call(
        functools.partial(_matmul_res_kernel, n_parts=n_parts),
        grid=(m // tm, n // tn),
        in_specs=in_specs,
        out_specs=pl.BlockSpec((tm, tn), lambda i, j: (i, j)),
        out_shape=jax.ShapeDtypeStruct((m, n), F32),
        compiler_params=_cparams("parallel", "arbitrary"),
        name="matmul_res",
    )(*a_parts, *w_parts, res)


def _conv_a_kernel(val_ref, glu_ref, gate_ref, *rest, tl, has_prefix, row_blk, ln_blk):
    if has_prefix:
        pre_ref, w_ref, b_ref, lg_ref, lb_ref, ya_ref, st_ref, ubuf, cbuf = rest
    else:
        w_ref, b_ref, lg_ref, lb_ref, ya_ref, st_ref, ubuf, cbuf = rest
    ch = val_ref.shape[1]
    off = CONV_PAD - (CONV_WIDTH - 1)

    @pl.when(pl.program_id(1) == 0)
    def _():
        if has_prefix:
            ubuf[0:CONV_PAD, :] = pre_ref[0]
        else:
            ubuf[0:CONV_PAD, :] = jnp.zeros((CONV_PAD, ch), F32)

    ubuf[CONV_PAD:CONV_PAD + tl, :] = val_ref[...] * _sigmoid(glu_ref[...])

    def lane_body(c, carry):
        c0 = pl.multiple_of(c * LANES, LANES)
        taps = [w_ref[k:k + 1, pl.ds(c0, LANES)] for k in range(CONV_WIDTH)]
        bias = b_ref[:, pl.ds(c0, LANES)]
        for r0 in range(0, tl, row_blk):
            acc = jnp.zeros((row_blk, LANES), F32) + bias
            for k in range(CONV_WIDTH):
                acc = acc + ubuf[r0 + off + k:r0 + off + k + row_blk, pl.ds(c0, LANES)] * taps[k]
            cbuf[r0:r0 + row_blk, pl.ds(c0, LANES)] = acc
        return carry

    lax.fori_loop(0, ch // LANES, lane_body, 0)

    tail = ubuf[tl:tl + CONV_PAD, :]
    st_ref[0] = tail
    ubuf[0:CONV_PAD, :] = tail

    def ln_body(r, carry):
        r0 = pl.multiple_of(r * ln_blk, ln_blk)
        u = cbuf[pl.ds(r0, ln_blk), :]
        mu = jnp.mean(u, axis=-1, keepdims=True)
        uc = u - mu
        var = jnp.mean(uc * uc, axis=-1, keepdims=True)
        y = uc * lax.rsqrt(var + EPS) * lg_ref[...] + lb_ref[...]
        ya = _silu(y) * _silu(gate_ref[pl.ds(r0, ln_blk), :])
        ya_ref[pl.ds(r0, ln_blk), :] = ya.astype(ya_ref.dtype)
        return carry

    lax.fori_loop(0, tl // ln_blk, ln_body, 0)


def _conv_a(proj, nb, seq, prefix, w, b, lg, lb, *, tl, out_dtype):
    ch = w.shape[1]
    nl = seq // tl
    has_prefix = prefix is not None
    row_blk = min(tl, 64)
    ln_blk = min(tl, 16)
    in_specs = [pl.BlockSpec((tl, ch), lambda bi, i: (bi * nl + i, 0)),
                pl.BlockSpec((tl, ch), lambda bi, i: (bi * nl + i, 1)),
                pl.BlockSpec((tl, ch), lambda bi, i: (bi * nl + i, 2))]
    args = [proj, proj, proj]
    if has_prefix:
        in_specs.append(pl.BlockSpec((1, CONV_PAD, ch), lambda bi, i: (bi, 0, 0)))
        args.append(prefix)
    in_specs += [pl.BlockSpec((CONV_WIDTH, ch), lambda bi, i: (0, 0))] + [pl.BlockSpec((1, ch), lambda bi, i: (0, 0))] * 3
    args += [w, b.reshape(1, ch), lg.reshape(1, ch), lb.reshape(1, ch)]
    ya, st = pl.pallas_call(
        functools.partial(_conv_a_kernel, tl=tl, has_prefix=has_prefix, row_blk=row_blk, ln_blk=ln_blk),
        grid=(nb, nl),
        in_specs=in_specs,
        out_specs=[pl.BlockSpec((tl, ch), lambda bi, i: (bi * nl + i, 0)),
                   pl.BlockSpec((1, CONV_PAD, ch), lambda bi, i: (bi, 0, 0))],
        out_shape=[jax.ShapeDtypeStruct((nb * seq, ch), out_dtype),
                   jax.ShapeDtypeStruct((nb, CONV_PAD, ch), F32)],
        scratch_shapes=[pltpu.VMEM((CONV_PAD + tl, ch), F32), pltpu.VMEM((tl, ch), F32)],
        compiler_params=_cparams("parallel", "arbitrary"),
        name="conv_a",
    )(*args)
    return ya, st[:, CONV_PAD - (CONV_WIDTH - 1):, :]


def _ssd_kernel(z_ref, xs_ref, bc_ref, dt_ref, *rest, q, qp, has_state):
    if has_state:
        pre_ref, h0_ref = rest[:2]
        rest = rest[2:]
    (cw_ref, cb_ref, dtb_ref, alog_ref, dsk_ref, ng_ref, e_ref, et_ref,
     yb_ref, cst_ref, hst_ref, xbuf, h_ref) = rest
    inner = xs_ref.shape[1]
    nbc = bc_ref.shape[1]
    xbc_w = inner + nbc
    gw = inner // SSD_GROUPS
    hpg = gw // SSD_HEAD_DIM
    off = SSD_PAD - (SSD_CONV - 1)

    @pl.when(pl.program_id(1) == 0)
    def _():
        if has_state:
            xbuf[0:SSD_PAD, :] = pre_ref[0]
            h_ref[...] = h0_ref[0]
        else:
            xbuf[0:SSD_PAD, :] = jnp.zeros((SSD_PAD, xbc_w), F32)
            h_ref[...] = jnp.zeros(h_ref.shape, F32)

    xbuf[SSD_PAD:SSD_PAD + q, 0:inner] = xs_ref[...]
    xbuf[SSD_PAD:SSD_PAD + q, inner:xbc_w] = bc_ref[...]
    conv = jnp.zeros((q, xbc_w), F32) + cb_ref[...]
    for k in range(SSD_CONV):
        conv = conv + xbuf[off + k:off + k + q, :] * cw_ref[k:k + 1, :]
    tail = xbuf[q:q + SSD_PAD, :]
    cst_ref[0] = tail
    xbuf[0:SSD_PAD, :] = tail
    xbc = _silu(conv)

    dt_raw = dt_ref[...] + dtb_ref[...]
    dt = jnp.maximum(dt_raw, 0.0) + jnp.log(1.0 + jnp.exp(-jnp.abs(dt_raw)))
    zg = z_ref[...]
    if q < qp:
        xbc = jnp.concatenate([xbc, jnp.zeros((qp - q, xbc_w), F32)], axis=0)
        dt = jnp.concatenate([dt, jnp.zeros((qp - q, LANES), F32)], axis=0)
        zg = jnp.concatenate([zg, jnp.zeros((qp - q, inner), F32)], axis=0)
    x = xbc[:, 0:inner]

    a_neg = -jnp.exp(alog_ref[...])
    dta = dt * a_neg
    row = lax.broadcasted_iota(jnp.int32, (qp, qp), 0)
    col = lax.broadcasted_iota(jnp.int32, (qp, qp), 1)
    causal = row >= col
    tril = jnp.where(causal, 1.0, 0.0).astype(BF16)
    acs = sum(_dot(tril, part) for part in _split3(dta))
    acs_t = acs.T
    last = acs[qp - 1:qp, :]
    eat = jnp.exp(acs)
    ddt = jnp.exp(last - acs) * dt

    stack = jnp.concatenate([dt, ddt, eat], axis=0)
    ex = sum(_dot(part, e_ref[...]) for part in _split2(stack))
    xdt = (x * ex[0:qp]).astype(BF16)
    xdd = (x * ex[qp:2 * qp]).astype(BF16)
    eat_x = ex[2 * qp:3 * qp]

    lane = lax.broadcasted_iota(jnp.int32, (qp, LANES), 1)
    lo_half = lane < SSD_HEAD_DIM
    y_parts, s_parts = [], []
    for g in range(SSD_GROUPS):
        bg = xbc[:, inner + g * SSD_STATE:inner + (g + 1) * SSD_STATE].astype(BF16)
        cg = xbc[:, inner + (SSD_GROUPS + g) * SSD_STATE:inner + (SSD_GROUPS + g + 1) * SSD_STATE].astype(BF16)
        cb = _dot_nt(cg, bg)
        hg = h_ref[g * gw:(g + 1) * gw, :].astype(BF16)
        y_off = _dot_nt(cg, hg) * eat_x[:, g * gw:(g + 1) * gw]
        s_parts.append(_dot_tn(xdd[:, g * gw:(g + 1) * gw], bg))
        yd = []
        for j in range(hpg // 2):
            h0 = g * hpg + 2 * j
            ms = []
            for h in (h0, h0 + 1):
                seg = jnp.where(causal, acs[:, h:h + 1] - acs_t[h:h + 1, :], -jnp.inf)
                ms.append(cb * jnp.exp(seg))
            lhs = jnp.concatenate(ms, axis=1).astype(BF16)
            xp = xdt[:, h0 * SSD_HEAD_DIM:h0 * SSD_HEAD_DIM + LANES]
            zero = jnp.zeros_like(xp)
            rhs = jnp.concatenate([jnp.where(lo_half, xp, zero), jnp.where(lo_half, zero, xp)], axis=0)
            yd.append(_dot(lhs, rhs))
        y_parts.append(jnp.concatenate(yd, axis=1) + y_off)
    y = jnp.concatenate(y_parts, axis=1) + dsk_ref[...] * x

    dec = jnp.broadcast_to(jnp.exp(acs_t[:, qp - 1:qp]), (LANES, LANES))
    dec_rows = sum(_dot(et_ref[...], part) for part in _split2(dec))
    h_new = h_ref[...] * dec_rows + jnp.concatenate(s_parts, axis=0)
    h_ref[...] = h_new
    hst_ref[0] = h_new

    y = y * _silu(zg)
    outs = []
    for g in range(SSD_GROUPS):
        yg = y[:, g * gw:(g + 1) * gw]
        outs.append(yg * lax.rsqrt(jnp.mean(yg * yg, axis=-1, keepdims=True) + EPS))
    yn = jnp.concatenate(outs, axis=1) * ng_ref[...]
    yb_ref[...] = yn[0:q].astype(yb_ref.dtype)


def _ssd(proj, dtp, nb, seq, prefix, h0, cw, cb, dt_bias, a_log, d_skip, norm_g, *, col0, q, out_dtype):
    inner = norm_g.shape[0]
    heads = dt_bias.shape[0]
    nbc = 2 * SSD_GROUPS * SSD_STATE
    xbc_w = inner + nbc
    nc = seq // q
    qp = SSD_CHUNK
    has_state = h0 is not None
    zb, xb, bcb = col0 // inner, col0 // inner + 1, (col0 + 2 * inner) // nbc
    assert col0 % inner == 0 and (col0 + 2 * inner) % nbc == 0

    lane_head = jnp.arange(inner, dtype=jnp.int32) // SSD_HEAD_DIM
    expand = (jnp.arange(LANES, dtype=jnp.int32)[:, None] == lane_head[None, :]).astype(BF16)
    pad = LANES - heads
    dtb = jnp.pad(dt_bias.astype(F32), (0, pad)).reshape(1, LANES)
    alog = jnp.pad(a_log.astype(F32), (0, pad)).reshape(1, LANES)
    dsk = jnp.repeat(d_skip.astype(F32), SSD_HEAD_DIM).reshape(1, inner)

    in_specs = [pl.BlockSpec((q, inner), lambda bi, c: (bi * nc + c, zb)),
                pl.BlockSpec((q, inner), lambda bi, c: (bi * nc + c, xb)),
                pl.BlockSpec((q, nbc), lambda bi, c: (bi * nc + c, bcb)),
                pl.BlockSpec((q, LANES), lambda bi, c: (bi * nc + c, 0))]
    args = [proj, proj, proj, dtp]
    if has_state:
        in_specs += [pl.BlockSpec((1, SSD_PAD, xbc_w), lambda bi, c: (bi, 0, 0)),
                     pl.BlockSpec((1, inner, SSD_STATE), lambda bi, c: (bi, 0, 0))]
        args += [prefix, h0]
    const = lambda bi, c: (0, 0)
    in_specs += [pl.BlockSpec((SSD_CONV, xbc_w), const), pl.BlockSpec((1, xbc_w), const),
                 pl.BlockSpec((1, LANES), const), pl.BlockSpec((1, LANES), const),
                 pl.BlockSpec((1, inner), const), pl.BlockSpec((1, inner), const),
                 pl.BlockSpec((LANES, inner), const), pl.BlockSpec((inner, LANES), const)]
    args += [cw, cb.reshape(1, xbc_w), dtb, alog, dsk, norm_g.reshape(1, inner), expand, expand.T]
    yb, cst, hst = pl.pallas_call(
        functools.partial(_ssd_kernel, q=q, qp=qp, has_state=has_state),
        grid=(nb, nc),
        in_specs=in_specs,
        out_specs=[pl.BlockSpec((q, inner), lambda bi, c: (bi * nc + c, 0)),
                   pl.BlockSpec((1, SSD_PAD, xbc_w), lambda bi, c: (bi, 0, 0)),
                   pl.BlockSpec((1, inner, SSD_STATE), lambda bi, c: (bi, 0, 0))],
        out_shape=[jax.ShapeDtypeStruct((nb * seq, inner), out_dtype),
                   jax.ShapeDtypeStruct((nb, SSD_PAD, xbc_w), F32),
                   jax.ShapeDtypeStruct((nb, inner, SSD_STATE), F32)],
        scratch_shapes=[pltpu.VMEM((SSD_PAD + q, xbc_w), F32), pltpu.VMEM((inner, SSD_STATE), F32)],
        compiler_params=_cparams("parallel", "arbitrary"),
        name="ssd",
    )(*args)
    return yb, cst[:, SSD_PAD - (SSD_CONV - 1):, :], hst


def _seg_rms(xb, gain, bd):
    ss = sum(_dot(part, bd) for part in _split2(xb * xb))
    return xb * lax.rsqrt(ss * (1.0 / ATT_QK_DIM) + EPS) * gain


def _qk_prep_kernel(q_ref, k_ref, qg_ref, kg_ref, bd_ref, qo_ref, ko_ref, *, block_diag):
    rows, width = q_ref.shape
    bd = bd_ref[...]
    scale = ATT_QK_DIM ** -0.5
    qn_parts = []
    for c in range(width // LANES):
        cs = slice(c * LANES, (c + 1) * LANES)
        ko_ref[:, cs] = _seg_rms(k_ref[:, cs], kg_ref[...], bd)
        qn = _seg_rms(q_ref[:, cs], qg_ref[...], bd) * scale
        if block_diag:
            qn_parts.append(qn)
        else:
            qo_ref[:, cs] = qn.astype(qo_ref.dtype)
    if block_diag:
        qn = jnp.concatenate(qn_parts, axis=1)
        seg = lax.broadcasted_iota(jnp.int32, (rows, width), 1) // ATT_QK_DIM
        blocks = [jnp.where(seg == s, qn, 0.0) for s in range(width // ATT_QK_DIM)]
        qo_ref[...] = jnp.concatenate(blocks, axis=0).astype(qo_ref.dtype)


def _qk_prep(proj, q_g, k_g, width, *, tm, block_diag):
    m = proj.shape[0]
    nseg = width // ATT_QK_DIM
    seg = jnp.arange(LANES, dtype=jnp.int32) // ATT_QK_DIM
    bd = (seg[:, None] == seg[None, :]).astype(BF16)
    qg = jnp.tile(q_g.astype(F32), LANES // ATT_QK_DIM).reshape(1, LANES)
    kg = jnp.tile(k_g.astype(F32), LANES // ATT_QK_DIM).reshape(1, LANES)
    q_rows = tm * nseg if block_diag else tm
    const = lambda i: (0, 0)
    return pl.pallas_call(
        functools.partial(_qk_prep_kernel, block_diag=block_diag),
        grid=(m // tm,),
        in_specs=[pl.BlockSpec((tm, width), lambda i: (i, 0)), pl.BlockSpec((tm, width), lambda i: (i, 1)),
                  pl.BlockSpec((1, LANES), const), pl.BlockSpec((1, LANES), const), pl.BlockSpec((LANES, LANES), const)],
        out_specs=[pl.BlockSpec((q_rows, width), lambda i: (i, 0)), pl.BlockSpec((tm, width), lambda i: (i, 0))],
        out_shape=[jax.ShapeDtypeStruct((m // tm * q_rows, width), BF16), jax.ShapeDtypeStruct((m, width), F32)],
        compiler_params=_cparams("parallel"),
        name="qk_prep",
    )(proj, proj, qg, kg, bd)


def _diff_out(o1, o2, lam, lam_init, sub_g, gate):
    d = o1 - lam * o2
    d = d * lax.rsqrt(jnp.mean(d * d, axis=-1, keepdims=True) + EPS) * sub_g * (1.0 - lam_init)
    return d * _silu(gate)


def _prompt_attn_kernel(lam_ref, q_ref, k_ref, v_ref, gate_ref, sg_ref, o_ref, kb_ref, vb_ref, *, tq, lam_init):
    seq = q_ref.shape[0]
    lam = lam_ref[0]
    kb_ref[...] = k_ref[...].astype(BF16)
    vb_ref[...] = v_ref[...].astype(BF16)
    lane = lax.broadcasted_iota(jnp.int32, (tq, LANES), 1)
    lo_half = lane < ATT_QK_DIM
    row = lax.broadcasted_iota(jnp.int32, (2 * tq, tq), 0)
    col = lax.broadcasted_iota(jnp.int32, (2 * tq, tq), 1)
    diag_ok = col <= jnp.where(row >= tq, row - tq, row)
    for qi in range(seq // tq):
        q = q_ref[qi * tq:(qi + 1) * tq, :]
        zero = jnp.zeros_like(q)
        q2 = jnp.concatenate([jnp.where(lo_half, q, zero), jnp.where(lo_half, zero, q)], axis=0)
        m = jnp.full((2 * tq, 1), NEG, F32)
        l = jnp.zeros((2 * tq, 1), F32)
        acc = jnp.zeros((2 * tq, LANES), F32)
        for ki in range(qi + 1):
            s = _dot_nt(q2, kb_ref[ki * tq:(ki + 1) * tq, :])
            if ki == qi:
                s = jnp.where(diag_ok, s, NEG)
            m_new = jnp.maximum(m, jnp.max(s, axis=-1, keepdims=True))
            alpha = jnp.exp(m - m_new)
            p = jnp.exp(s - m_new)
            l = alpha * l + jnp.sum(p, axis=-1, keepdims=True)
            acc = alpha * acc + _dot(p.astype(BF16), vb_ref[ki * tq:(ki + 1) * tq, :])
            m = m_new
        o = acc / l
        out = _diff_out(o[0:tq], o[tq:2 * tq], lam, lam_init, sg_ref[...], gate_ref[qi * tq:(qi + 1) * tq, :])
        o_ref[qi * tq:(qi + 1) * tq, :] = out.astype(o_ref.dtype)


def _prompt_attn(lam, qn, kn, proj, sub_g, nb, seq, heads, *, v_blk, gate_blk, lam_init, tq):
    m = nb * seq
    width = heads * ATT_V_DIM
    blk = lambda off: pl.BlockSpec((seq, LANES), lambda bi, h: (bi, off + h))
    return pl.pallas_call(
        functools.partial(_prompt_attn_kernel, tq=tq, lam_init=lam_init),
        grid=(nb, heads),
        in_specs=[pl.BlockSpec(memory_space=pltpu.SMEM), blk(0), blk(0), blk(v_blk), blk(gate_blk),
                  pl.BlockSpec((1, LANES), lambda bi, h: (0, 0))],
        out_specs=blk(0),
        out_shape=jax.ShapeDtypeStruct((m, width), BF16),
        scratch_shapes=[pltpu.VMEM((seq, LANES), BF16), pltpu.VMEM((seq, LANES), BF16)],
        compiler_params=_cparams("parallel", "parallel"),
        name="prompt_attn",
    )(lam, qn, kn, proj, proj, sub_g.reshape(1, LANES))


def _sample_attn_kernel(pt_ref, lam_ref, qbd_ref, kc_ref, vc_ref, kn_ref, vn_ref, gate_ref, sg_ref, o_ref,
                        m_ref, l_ref, acc_ref, *, heads, dec, lam_init):
    p = pl.program_id(1)
    rph = 2 * dec
    rows = heads * rph

    @pl.when(p == 0)
    def _():
        m_ref[...] = jnp.full(m_ref.shape, NEG, F32)
        l_ref[...] = jnp.zeros(l_ref.shape, F32)
        acc_ref[...] = jnp.zeros(acc_ref.shape, F32)

    def update(kpage, vpage, mask):
        s = _dot_nt(qbd_ref[...], kpage)
        if mask is not None:
            s = jnp.where(mask, s, NEG)
        m_new = jnp.maximum(m_ref[...], jnp.max(s, axis=-1, keepdims=True))
        alpha = jnp.exp(m_ref[...] - m_new)
        pr = jnp.exp(s - m_new)
        l_ref[...] = alpha * l_ref[...] + jnp.sum(pr, axis=-1, keepdims=True)
        prb = pr.astype(BF16)
        pv = [_dot(prb[h * rph:(h + 1) * rph, :], vpage[:, h * ATT_V_DIM:(h + 1) * ATT_V_DIM]) for h in range(heads)]
        acc_ref[...] = alpha * acc_ref[...] + jnp.concatenate(pv, axis=0)
        m_ref[...] = m_new

    update(kc_ref[0].astype(BF16), vc_ref[0].astype(BF16), None)

    @pl.when(p == pl.num_programs(1) - 1)
    def _():
        width = kn_ref.shape[1]
        zpad = jnp.zeros((PAGE - dec, width), F32)
        kpage = jnp.concatenate([kn_ref[...], zpad], axis=0).astype(BF16)
        vpage = jnp.concatenate([vn_ref[...], zpad], axis=0).astype(BF16)
        row = lax.broadcasted_iota(jnp.int32, (rows, PAGE), 0)
        col = lax.broadcasted_iota(jnp.int32, (rows, PAGE), 1)
        update(kpage, vpage, col <= row % dec)
        o = acc_ref[...] / l_ref[...]
        lam = lam_ref[0]
        outs = []
        for h in range(heads):
            o1 = o[h * rph:h * rph + dec]
            o2 = o[h * rph + dec:(h + 1) * rph]
            outs.append(_diff_out(o1, o2, lam, lam_init, sg_ref[...], gate_ref[:, h * ATT_V_DIM:(h + 1) * ATT_V_DIM]))
        o_ref[...] = jnp.concatenate(outs, axis=1)


def _sample_attn(page_table, lam, qbd, cache_k, cache_v, kn, proj, sub_g, nb, dec, heads, *, page0, v_blk, gate_blk,
                 lam_init):
    width = heads * ATT_V_DIM
    rows = heads * 2 * dec
    n_pages = page_table.shape[1]
    grid_spec = pltpu.PrefetchScalarGridSpec(
        num_scalar_prefetch=1,
        grid=(nb, n_pages),
        in_specs=[pl.BlockSpec(memory_space=pltpu.SMEM),
                  pl.BlockSpec((rows, width), lambda bi, p, pt: (bi, 0)),
                  pl.BlockSpec((1, PAGE, width), lambda bi, p, pt: (page0 + pt[bi, p], 0, 0)),
                  pl.BlockSpec((1, PAGE, width), lambda bi, p, pt: (page0 + pt[bi, p], 0, 0)),
                  pl.BlockSpec((dec, width), lambda bi, p, pt: (bi, 0)),
                  pl.BlockSpec((dec, width), lambda bi, p, pt: (bi, v_blk)),
                  pl.BlockSpec((dec, width), lambda bi, p, pt: (bi, gate_blk)),
                  pl.BlockSpec((1, LANES), lambda bi, p, pt: (0, 0))],
        out_specs=pl.BlockSpec((dec, width), lambda bi, p, pt: (bi, 0)),
        scratch_shapes=[pltpu.VMEM((rows, 1), F32), pltpu.VMEM((rows, 1), F32), pltpu.VMEM((rows, ATT_V_DIM), F32)],
    )
    return pl.pallas_call(
        functools.partial(_sample_attn_kernel, heads=heads, dec=dec, lam_init=lam_init),
        grid_spec=grid_spec,
        out_shape=jax.ShapeDtypeStruct((nb * dec, width), F32),
        compiler_params=_cparams("parallel", "arbitrary"),
        name="sample_attn",
    )(page_table, lam, qbd, cache_k, cache_v, kn, proj, proj, sub_g.reshape(1, LANES))


def _even_layer(x, nb, seq, conv_prev, ssd_conv_prev, ssd_prev, p, *, tm, tl, q, act_dtype):
    ch = p["conv_w"].shape[1]
    proj, dtp = _norm_matmul(x, p["norm_g"], p["w_main"], p["w_dt"], tm=tm, tn=1024)
    if conv_prev is not None:
        conv_prev = jnp.pad(conv_prev, ((0, 0), (CONV_PAD - (CONV_WIDTH - 1), 0), (0, 0)))
        ssd_conv_prev = jnp.pad(ssd_conv_prev, ((0, 0), (SSD_PAD - (SSD_CONV - 1), 0), (0, 0)))
        ssd_prev = ssd_prev.reshape(nb, -1, SSD_STATE)
    ya, conv_new = _conv_a(proj, nb, seq, conv_prev, p["conv_w"], p["conv_b"], p["ln_g"], p["ln_b"],
                           tl=tl, out_dtype=act_dtype)
    yb, ssd_conv_new, ssd_new = _ssd(proj, dtp, nb, seq, ssd_conv_prev, ssd_prev, p["ssd_conv_w"], p["ssd_conv_b"],
                                     p["dt_bias"], p["a_log"], p["d_skip"], p["ssd_norm_g"],
                                     col0=3 * ch, q=q, out_dtype=act_dtype)
    y = _matmul_res([ya, yb], [p["w_out_a"], p["w_out_b"]], x, tm=tm, tn=512)
    return y, conv_new, ssd_conv_new, ssd_new


def kernel(x_prompt, x_sample, state_conv_a, state_ssd_conv, state_ssd, cache_k, cache_v, page_table, norm_even, w_in_even, conv_a_w, conv_a_b, conv_a_ln_g, conv_a_ln_b, ssd_conv_w, ssd_conv_b, ssd_dt_bias, ssd_a_log, ssd_d, ssd_norm_g, w_out_even, norm_odd, w_in_odd, q_norm_g, k_norm_g, lam_q1, lam_k1, lam_q2, lam_k2, subln_g, w_out_odd):
    bp, seq, d = x_prompt.shape
    bs, dec, _ = x_sample.shape
    depth = norm_even.shape[0] + norm_odd.shape[0]
    xp = x_prompt.reshape(bp * seq, d)
    xs = x_sample.reshape(bs * dec, d)
    ch = conv_a_w.shape[2]
    ssd_heads = ssd_dt_bias.shape[1]
    inner = ssd_norm_g.shape[1]
    att_width = w_out_odd.shape[1]
    heads = att_width // ATT_V_DIM
    main_cols = w_in_even.shape[2] - ssd_heads

    tm_p = min(1024, bp * seq)
    tq = min(512, seq)

    ca_p, ca_s, sc_p, sc_s, ss_p, ss_s = [], [], [], [], [], []
    k_p, k_s, v_p, v_s = [], [], [], []
    for layer in range(depth):
        j = layer // 2
        if layer % 2 == 0:
            w_in = w_in_even[j]
            p = dict(norm_g=norm_even[j],
                     w_main=w_in[:, :main_cols].astype(BF16),
                     w_dt=jnp.pad(w_in[:, main_cols:], ((0, 0), (0, LANES - ssd_heads))).astype(BF16),
                     conv_w=conv_a_w[j], conv_b=conv_a_b[j], ln_g=conv_a_ln_g[j], ln_b=conv_a_ln_b[j],
                     ssd_conv_w=ssd_conv_w[j], ssd_conv_b=ssd_conv_b[j], dt_bias=ssd_dt_bias[j], a_log=ssd_a_log[j],
                     d_skip=ssd_d[j], ssd_norm_g=ssd_norm_g[j],
                     w_out_a=w_out_even[j][:ch].astype(BF16), w_out_b=w_out_even[j][ch:].astype(BF16))
            xp, c1, c2, c3 = _even_layer(xp, bp, seq, None, None, None, p, tm=tm_p, tl=256, q=SSD_CHUNK, act_dtype=BF16)
            xs, d1, d2, d3 = _even_layer(xs, bs, dec, state_conv_a[j], state_ssd_conv[j], state_ssd[j], p,
                                         tm=bs * dec, tl=dec, q=dec, act_dtype=F32)
            ca_p.append(c1)
            sc_p.append(c2)
            ss_p.append(c3.reshape(bp, ssd_heads, SSD_HEAD_DIM, SSD_STATE))
            ca_s.append(d1)
            sc_s.append(d2)
            ss_s.append(d3.reshape(bs, ssd_heads, SSD_HEAD_DIM, SSD_STATE))
        else:
            lam_init = 0.8 - 0.6 * math.exp(-0.3 * layer)
            lam = (jnp.exp(jnp.sum(lam_q1[j].astype(F32) * lam_k1[j].astype(F32)))
                   - jnp.exp(jnp.sum(lam_q2[j].astype(F32) * lam_k2[j].astype(F32))) + lam_init).reshape(1)
            w_in = w_in_odd[j].astype(BF16)
            w_out = w_out_odd[j].astype(BF16)
            v_blk, gate_blk = 2 * att_width // LANES, 3 * att_width // LANES

            proj = _norm_matmul(xp, norm_odd[j], w_in, tm=tm_p, tn=1024)
            qn, kn = _qk_prep(proj, q_norm_g[j], k_norm_g[j], att_width, tm=min(512, tm_p), block_diag=False)
            o = _prompt_attn(lam, qn, kn, proj, subln_g[j], bp, seq, heads,
                             v_blk=v_blk, gate_blk=gate_blk, lam_init=lam_init, tq=tq)
            k_p.append(kn.reshape(bp, seq, heads, 2, ATT_QK_DIM))
            v_p.append(proj[:, 2 * att_width:3 * att_width].reshape(bp, seq, heads, ATT_V_DIM))
            xp = _matmul_res([o], [w_out], xp, tm=tm_p, tn=512)

            proj = _norm_matmul(xs, norm_odd[j], w_in, tm=bs * dec, tn=1024)
            qbd, kn = _qk_prep(proj, q_norm_g[j], k_norm_g[j], att_width, tm=dec, block_diag=True)
            n_pool = cache_k.shape[1]
            o = _sample_attn(page_table, lam, qbd, cache_k.reshape(-1, PAGE, att_width),
                             cache_v.reshape(-1, PAGE, att_width), kn, proj, subln_g[j], bs, dec, heads,
                             page0=j * n_pool, v_blk=2, gate_blk=3, lam_init=lam_init)
            k_s.append(kn.reshape(bs, dec, heads, 2, ATT_QK_DIM))
            v_s.append(proj[:, 2 * att_width:3 * att_width].reshape(bs, dec, heads, ATT_V_DIM))
            xs = _matmul_res([o], [w_out], xs, tm=bs * dec, tn=512)
    return (xp.reshape(bp, seq, d), xs.reshape(bs, dec, d), jnp.stack(ca_p), jnp.stack(ca_s), jnp.stack(sc_p),
            jnp.stack(sc_s), jnp.stack(ss_p), jnp.stack(ss_s), jnp.stack(k_p), jnp.stack(k_s), jnp.stack(v_p),
            jnp.stack(v_s))
```

```python
import functools
import math

import jax
import jax.numpy as jnp
from jax import lax
from jax.experimental import pallas as pl
from jax.experimental.pallas import tpu as pltpu

F32 = jnp.float32
BF16 = jnp.bfloat16

EPS = 1e-6
NEG = -1e30
LANES = 128
SUBLANES = 8
CONV_WIDTH = 31
CONV_PAD = 32
SSD_HEAD_DIM = 64
SSD_GROUPS = 4
SSD_STATE = 128
SSD_CONV = 4
SSD_PAD = 8
SSD_CHUNK = 128
ATT_QK_DIM = 64
ATT_V_DIM = 128
PAGE = 128
PAGE_GROUP = 4
VMEM_LIMIT = 56 * 1024 * 1024


def _cparams(*sem):
    return pltpu.CompilerParams(dimension_semantics=sem, vmem_limit_bytes=VMEM_LIMIT)


def _sigmoid(x):
    return 0.5 * jnp.tanh(0.5 * x) + 0.5


def _silu(x):
    return x * _sigmoid(x)


def _split2(x):
    hi = x.astype(BF16)
    lo = (x - hi.astype(F32)).astype(BF16)
    return hi, lo


def _split3(x):
    hi = x.astype(BF16)
    r = x - hi.astype(F32)
    mid = r.astype(BF16)
    lo = (r - mid.astype(F32)).astype(BF16)
    return hi, mid, lo


def _dot(a, b):
    return jnp.dot(a, b, preferred_element_type=F32)


def _dot_nt(a, b):
    return lax.dot_general(a, b, (((1,), (1,)), ((), ())), preferred_element_type=F32)


def _dot_tn(a, b):
    return lax.dot_general(a, b, (((0,), (0,)), ((), ())), preferred_element_type=F32)


def _norm_matmul_kernel(x_ref, g_ref, w_ref, *rest, has_aux):
    if has_aux:
        wa_ref, o_ref, oa_ref, hn_ref = rest
    else:
        o_ref, hn_ref = rest

    @pl.when(pl.program_id(1) == 0)
    def _():
        x = x_ref[...]
        ms = jnp.mean(x * x, axis=-1, keepdims=True)
        hn_ref[...] = (x * lax.rsqrt(ms + EPS) * g_ref[...]).astype(BF16)
        if has_aux:
            oa_ref[...] = _dot(hn_ref[...], wa_ref[...])

    o_ref[...] = _dot(hn_ref[...], w_ref[...])


def _norm_matmul(x, g, w, w_aux=None, *, tm, tn):
    m, k = x.shape
    n = w.shape[1]
    assert m % tm == 0 and n % tn == 0
    has_aux = w_aux is not None
    in_specs = [pl.BlockSpec((tm, k), lambda i, j: (i, 0)),
                pl.BlockSpec((1, k), lambda i, j: (0, 0)),
                pl.BlockSpec((k, tn), lambda i, j: (0, j))]
    out_specs = [pl.BlockSpec((tm, tn), lambda i, j: (i, j))]
    out_shape = [jax.ShapeDtypeStruct((m, n), F32)]
    args = [x, g.reshape(1, k), w]
    if has_aux:
        na = w_aux.shape[1]
        in_specs.append(pl.BlockSpec((k, na), lambda i, j: (0, 0)))
        out_specs.append(pl.BlockSpec((tm, na), lambda i, j: (i, 0)))
        out_shape.append(jax.ShapeDtypeStruct((m, na), F32))
        args.append(w_aux)
    outs = pl.pallas_call(
        functools.partial(_norm_matmul_kernel, has_aux=has_aux),
        grid=(m // tm, n // tn),
        in_specs=in_specs, out_specs=out_specs, out_shape=out_shape,
        scratch_shapes=[pltpu.VMEM((tm, k), BF16)],
        compiler_params=_cparams("parallel", "arbitrary"),
        name="norm_matmul",
    )(*args)
    return outs if has_aux else outs[0]


def _matmul_res_kernel(*refs, n_parts):
    a_refs = refs[:n_parts]
    w_refs = refs[n_parts:2 * n_parts]
    res_ref, o_ref = refs[2 * n_parts], refs[2 * n_parts + 1]
    acc = res_ref[...]
    for a_ref, w_ref in zip(a_refs, w_refs):
        acc = acc + _dot(a_ref[...].astype(BF16), w_ref[...])
    o_ref[...] = acc


def _matmul_res(a_parts, w_parts, res, *, tm, tn):
    m, n = res.shape
    assert m % tm == 0 and n % tn == 0
    n_parts = len(a_parts)
    in_specs = ([pl.BlockSpec((tm, a.shape[1]), lambda i, j: (i, 0)) for a in a_parts]
                + [pl.BlockSpec((w.shape[0], tn), lambda i, j: (0, j)) for w in w_parts]
                + [pl.BlockSpec((tm, tn), lambda i, j: (i, j))])
    return pl.pallas_call(
        functools.partial(_matmul_res_kernel, n_parts=n_parts),
        grid=(m // tm, n // tn),
        in_specs=in_specs,
        out_specs=pl.BlockSpec((tm, tn), lambda i, j: (i, j)),
        out_shape=jax.ShapeDtypeStruct((m, n), F32),
        compiler_params=_cparams("parallel", "arbitrary"),
        name="matmul_res",
    )(*a_parts, *w_parts, res)


def _conv_a_kernel(val_ref, glu_ref, gate_ref, *rest, tl, has_prefix, row_blk, ln_blk):
    if has_prefix:
        pre_ref, w_ref, b_ref, lg_ref, lb_ref, ya_ref, st_ref, ubuf, cbuf = rest
    else:
        w_ref, b_ref, lg_ref, lb_ref, ya_ref, st_ref, ubuf, cbuf = rest
    ch = val_ref.shape[1]
    off = CONV_PAD - (CONV_WIDTH - 1)

    @pl.when(pl.program_id(1) == 0)
    def _():
        if has_prefix:
            ubuf[0:CONV_PAD, :] = pre_ref[0]
        else:
            ubuf[0:CONV_PAD, :] = jnp.zeros((CONV_PAD, ch), F32)

    ubuf[CONV_PAD:CONV_PAD + tl, :] = val_ref[...] * _sigmoid(glu_ref[...])

    def lane_body(c, carry):
        c0 = pl.multiple_of(c * LANES, LANES)
        taps = [w_ref[k:k + 1, pl.ds(c0, LANES)] for k in range(CONV_WIDTH)]
        bias = b_ref[:, pl.ds(c0, LANES)]
        for r0 in range(0, tl, row_blk):
            acc = jnp.zeros((row_blk, LANES), F32) + bias
            for sh in range(SUBLANES):
                ks = list(range(sh, CONV_WIDTH, SUBLANES))
                span = row_blk + ks[-1] - sh
                win = ubuf[r0 + off + sh:r0 + off + sh + span, pl.ds(c0, LANES)]
                for k in ks:
                    acc = acc + win[k - sh:k - sh + row_blk] * taps[k]
            cbuf[r0:r0 + row_blk, pl.ds(c0, LANES)] = acc
        return carry

    lax.fori_loop(0, ch // LANES, lane_body, 0)

    tail = ubuf[tl:tl + CONV_PAD, :]
    st_ref[0] = tail
    ubuf[0:CONV_PAD, :] = tail

    def ln_body(r, carry):
        r0 = pl.multiple_of(r * ln_blk, ln_blk)
        u = cbuf[pl.ds(r0, ln_blk), :]
        mu = jnp.mean(u, axis=-1, keepdims=True)
        uc = u - mu
        var = jnp.mean(uc * uc, axis=-1, keepdims=True)
        y = uc * lax.rsqrt(var + EPS) * lg_ref[...] + lb_ref[...]
        ya = _silu(y) * _silu(gate_ref[pl.ds(r0, ln_blk), :])
        ya_ref[pl.ds(r0, ln_blk), :] = ya.astype(ya_ref.dtype)
        return carry

    lax.fori_loop(0, tl // ln_blk, ln_body, 0)


def _conv_a(proj, nb, seq, prefix, w, b, lg, lb, *, tl, out_dtype):
    ch = w.shape[1]
    nl = seq // tl
    has_prefix = prefix is not None
    row_blk = min(tl, 64)
    ln_blk = min(tl, 16)
    in_specs = [pl.BlockSpec((tl, ch), lambda bi, i: (bi * nl + i, 0)),
                pl.BlockSpec((tl, ch), lambda bi, i: (bi * nl + i, 1)),
                pl.BlockSpec((tl, ch), lambda bi, i: (bi * nl + i, 2))]
    args = [proj, proj, proj]
    if has_prefix:
        in_specs.append(pl.BlockSpec((1, CONV_PAD, ch), lambda bi, i: (bi, 0, 0)))
        args.append(prefix)
    in_specs += [pl.BlockSpec((CONV_WIDTH, ch), lambda bi, i: (0, 0))] + [pl.BlockSpec((1, ch), lambda bi, i: (0, 0))] * 3
    args += [w, b.reshape(1, ch), lg.reshape(1, ch), lb.reshape(1, ch)]
    ya, st = pl.pallas_call(
        functools.partial(_conv_a_kernel, tl=tl, has_prefix=has_prefix, row_blk=row_blk, ln_blk=ln_blk),
        grid=(nb, nl),
        in_specs=in_specs,
        out_specs=[pl.BlockSpec((tl, ch), lambda bi, i: (bi * nl + i, 0)),
                   pl.BlockSpec((1, CONV_PAD, ch), lambda bi, i: (bi, 0, 0))],
        out_shape=[jax.ShapeDtypeStruct((nb * seq, ch), out_dtype),
                   jax.ShapeDtypeStruct((nb, CONV_PAD, ch), F32)],
        scratch_shapes=[pltpu.VMEM((CONV_PAD + tl, ch), F32), pltpu.VMEM((tl, ch), F32)],
        compiler_params=_cparams("parallel", "arbitrary"),
        name="conv_a",
    )(*args)
    return ya, st[:, CONV_PAD - (CONV_WIDTH - 1):, :]


def _ssd_kernel(z_ref, xs_ref, bc_ref, dt_ref, *rest, q, qp, has_state):
    if has_state:
        pre_ref, h0_ref = rest[:2]
        rest = rest[2:]
    (cw_ref, cb_ref, dtb_ref, alog_ref, dsk_ref, ng_ref, e_ref, et_ref,
     yb_ref, cst_ref, hst_ref, xbuf, h_ref) = rest
    inner = xs_ref.shape[1]
    nbc = bc_ref.shape[1]
    xbc_w = inner + nbc
    gw = inner // SSD_GROUPS
    hpg = gw // SSD_HEAD_DIM
    off = SSD_PAD - (SSD_CONV - 1)

    @pl.when(pl.program_id(1) == 0)
    def _():
        if has_state:
            xbuf[0:SSD_PAD, :] = pre_ref[0]
            h_ref[...] = h0_ref[0]
        else:
            xbuf[0:SSD_PAD, :] = jnp.zeros((SSD_PAD, xbc_w), F32)
            h_ref[...] = jnp.zeros(h_ref.shape, F32)

    xbuf[SSD_PAD:SSD_PAD + q, 0:inner] = xs_ref[...]
    xbuf[SSD_PAD:SSD_PAD + q, inner:xbc_w] = bc_ref[...]
    conv = jnp.zeros((q, xbc_w), F32) + cb_ref[...]
    for k in range(SSD_CONV):
        conv = conv + xbuf[off + k:off + k + q, :] * cw_ref[k:k + 1, :]
    tail = xbuf[q:q + SSD_PAD, :]
    cst_ref[0] = tail
    xbuf[0:SSD_PAD, :] = tail
    xbc = _silu(conv)

    dt_raw = dt_ref[...] + dtb_ref[...]
    dt = jnp.maximum(dt_raw, 0.0) + jnp.log(1.0 + jnp.exp(-jnp.abs(dt_raw)))
    zg = z_ref[...]
    if q < qp:
        xbc = jnp.concatenate([xbc, jnp.zeros((qp - q, xbc_w), F32)], axis=0)
        dt = jnp.concatenate([dt, jnp.zeros((qp - q, LANES), F32)], axis=0)
        zg = jnp.concatenate([zg, jnp.zeros((qp - q, inner), F32)], axis=0)
    x = xbc[:, 0:inner]

    a_neg = -jnp.exp(alog_ref[...])
    dta = dt * a_neg
    row = lax.broadcasted_iota(jnp.int32, (qp, qp), 0)
    col = lax.broadcasted_iota(jnp.int32, (qp, qp), 1)
    causal = row >= col
    tril = jnp.where(causal, 1.0, 0.0).astype(BF16)
    acs = sum(_dot(tril, part) for part in _split3(dta))
    acs_t = acs.T
    last = acs[qp - 1:qp, :]
    eat = jnp.exp(acs)
    ddt = jnp.exp(last - acs) * dt

    stack = jnp.concatenate([dt, ddt, eat], axis=0)
    ex = sum(_dot(part, e_ref[...]) for part in _split2(stack))
    xdt = (x * ex[0:qp]).astype(BF16)
    xdd = (x * ex[qp:2 * qp]).astype(BF16)
    eat_x = ex[2 * qp:3 * qp]

    lane = lax.broadcasted_iota(jnp.int32, (qp, LANES), 1)
    lo_half = lane < SSD_HEAD_DIM
    y_parts, s_parts = [], []
    for g in range(SSD_GROUPS):
        bg = xbc[:, inner + g * SSD_STATE:inner + (g + 1) * SSD_STATE].astype(BF16)
        cg = xbc[:, inner + (SSD_GROUPS + g) * SSD_STATE:inner + (SSD_GROUPS + g + 1) * SSD_STATE].astype(BF16)
        cb = _dot_nt(cg, bg)
        hg = h_ref[g * gw:(g + 1) * gw, :].astype(BF16)
        y_off = _dot_nt(cg, hg) * eat_x[:, g * gw:(g + 1) * gw]
        s_parts.append(_dot_tn(xdd[:, g * gw:(g + 1) * gw], bg))
        yd = []
        for j in range(hpg // 2):
            h0 = g * hpg + 2 * j
            ms = []
            for h in (h0, h0 + 1):
                seg = jnp.where(causal, acs[:, h:h + 1] - acs_t[h:h + 1, :], -jnp.inf)
                ms.append(cb * jnp.exp(seg))
            lhs = jnp.concatenate(ms, axis=1).astype(BF16)
            xp = xdt[:, h0 * SSD_HEAD_DIM:h0 * SSD_HEAD_DIM + LANES]
            zero = jnp.zeros_like(xp)
            rhs = jnp.concatenate([jnp.where(lo_half, xp, zero), jnp.where(lo_half, zero, xp)], axis=0)
            yd.append(_dot(lhs, rhs))
        y_parts.append(jnp.concatenate(yd, axis=1) + y_off)
    y = jnp.concatenate(y_parts, axis=1) + dsk_ref[...] * x

    dec = jnp.broadcast_to(jnp.exp(acs_t[:, qp - 1:qp]), (LANES, LANES))
    dec_rows = sum(_dot(et_ref[...], part) for part in _split2(dec))
    h_new = h_ref[...] * dec_rows + jnp.concatenate(s_parts, axis=0)
    h_ref[...] = h_new
    hst_ref[0] = h_new

    y = y * _silu(zg)
    outs = []
    for g in range(SSD_GROUPS):
        yg = y[:, g * gw:(g + 1) * gw]
        outs.append(yg * lax.rsqrt(jnp.mean(yg * yg, axis=-1, keepdims=True) + EPS))
    yn = jnp.concatenate(outs, axis=1) * ng_ref[...]
    yb_ref[...] = yn[0:q].astype(yb_ref.dtype)


def _ssd(proj, dtp, nb, seq, prefix, h0, cw, cb, dt_bias, a_log, d_skip, norm_g, *, col0, q, out_dtype):
    inner = norm_g.shape[0]
    heads = dt_bias.shape[0]
    nbc = 2 * SSD_GROUPS * SSD_STATE
    xbc_w = inner + nbc
    nc = seq // q
    qp = SSD_CHUNK
    has_state = h0 is not None
    zb, xb, bcb = col0 // inner, col0 // inner + 1, (col0 + 2 * inner) // nbc
    assert col0 % inner == 0 and (col0 + 2 * inner) % nbc == 0

    lane_head = jnp.arange(inner, dtype=jnp.int32) // SSD_HEAD_DIM
    expand = (jnp.arange(LANES, dtype=jnp.int32)[:, None] == lane_head[None, :]).astype(BF16)
    pad = LANES - heads
    dtb = jnp.pad(dt_bias.astype(F32), (0, pad)).reshape(1, LANES)
    alog = jnp.pad(a_log.astype(F32), (0, pad)).reshape(1, LANES)
    dsk = jnp.repeat(d_skip.astype(F32), SSD_HEAD_DIM).reshape(1, inner)

    in_specs = [pl.BlockSpec((q, inner), lambda bi, c: (bi * nc + c, zb)),
                pl.BlockSpec((q, inner), lambda bi, c: (bi * nc + c, xb)),
                pl.BlockSpec((q, nbc), lambda bi, c: (bi * nc + c, bcb)),
                pl.BlockSpec((q, LANES), lambda bi, c: (bi * nc + c, 0))]
    args = [proj, proj, proj, dtp]
    if has_state:
        in_specs += [pl.BlockSpec((1, SSD_PAD, xbc_w), lambda bi, c: (bi, 0, 0)),
                     pl.BlockSpec((1, inner, SSD_STATE), lambda bi, c: (bi, 0, 0))]
        args += [prefix, h0]
    const = lambda bi, c: (0, 0)
    in_specs += [pl.BlockSpec((SSD_CONV, xbc_w), const), pl.BlockSpec((1, xbc_w), const),
                 pl.BlockSpec((1, LANES), const), pl.BlockSpec((1, LANES), const),
                 pl.BlockSpec((1, inner), const), pl.BlockSpec((1, inner), const),
                 pl.BlockSpec((LANES, inner), const), pl.BlockSpec((inner, LANES), const)]
    args += [cw, cb.reshape(1, xbc_w), dtb, alog, dsk, norm_g.reshape(1, inner), expand, expand.T]
    yb, cst, hst = pl.pallas_call(
        functools.partial(_ssd_kernel, q=q, qp=qp, has_state=has_state),
        grid=(nb, nc),
        in_specs=in_specs,
        out_specs=[pl.BlockSpec((q, inner), lambda bi, c: (bi * nc + c, 0)),
                   pl.BlockSpec((1, SSD_PAD, xbc_w), lambda bi, c: (bi, 0, 0)),
                   pl.BlockSpec((1, inner, SSD_STATE), lambda bi, c: (bi, 0, 0))],
        out_shape=[jax.ShapeDtypeStruct((nb * seq, inner), out_dtype),
                   jax.ShapeDtypeStruct((nb, SSD_PAD, xbc_w), F32),
                   jax.ShapeDtypeStruct((nb, inner, SSD_STATE), F32)],
        scratch_shapes=[pltpu.VMEM((SSD_PAD + q, xbc_w), F32), pltpu.VMEM((inner, SSD_STATE), F32)],
        compiler_params=_cparams("parallel", "arbitrary"),
        name="ssd",
    )(*args)
    return yb, cst[:, SSD_PAD - (SSD_CONV - 1):, :], hst


def _seg_rms(xb, gain, bd):
    ss = sum(_dot(part, bd) for part in _split2(xb * xb))
    return xb * lax.rsqrt(ss * (1.0 / ATT_QK_DIM) + EPS) * gain


Q_SCALE = ATT_QK_DIM ** -0.5 * math.log2(math.e)


def _qk_prep_prompt_kernel(q_ref, k_ref, qg_ref, kg_ref, bd_ref, qo_ref, kt_ref):
    width = q_ref.shape[1]
    bd = bd_ref[...]
    for c in range(width // LANES):
        cs = slice(c * LANES, (c + 1) * LANES)
        kt_ref[0, cs, :] = _seg_rms(k_ref[:, cs], kg_ref[...], bd).T
        qo_ref[:, cs] = (_seg_rms(q_ref[:, cs], qg_ref[...], bd) * Q_SCALE).astype(qo_ref.dtype)


def _qk_prep_sample_kernel(q_ref, k_ref, qg_ref, kg_ref, bd_ref, qo_ref, ko_ref):
    rows, width = q_ref.shape
    bd = bd_ref[...]
    lo_half = lax.broadcasted_iota(jnp.int32, (rows, LANES), 1) < ATT_QK_DIM
    blocks = []
    for c in range(width // LANES):
        cs = slice(c * LANES, (c + 1) * LANES)
        ko_ref[:, cs] = _seg_rms(k_ref[:, cs], kg_ref[...], bd)
        qn = _seg_rms(q_ref[:, cs], qg_ref[...], bd) * Q_SCALE
        blocks += [jnp.where(lo_half, qn, 0.0), jnp.where(lo_half, 0.0, qn)]
    qo_ref[...] = jnp.concatenate(blocks, axis=0).astype(qo_ref.dtype)


def _qk_consts(q_g, k_g):
    seg = jnp.arange(LANES, dtype=jnp.int32) // ATT_QK_DIM
    bd = (seg[:, None] == seg[None, :]).astype(BF16)
    qg = jnp.tile(q_g.astype(F32), LANES // ATT_QK_DIM).reshape(1, LANES)
    kg = jnp.tile(k_g.astype(F32), LANES // ATT_QK_DIM).reshape(1, LANES)
    return qg, kg, bd


def _qk_prep_prompt(proj, q_g, k_g, nb, seq, width, *, tm):
    nl = seq // tm
    const = lambda i: (0, 0)
    return pl.pallas_call(
        _qk_prep_prompt_kernel,
        grid=(nb * nl,),
        in_specs=[pl.BlockSpec((tm, width), lambda i: (i, 0)), pl.BlockSpec((tm, width), lambda i: (i, 1)),
                  pl.BlockSpec((1, LANES), const), pl.BlockSpec((1, LANES), const), pl.BlockSpec((LANES, LANES), const)],
        out_specs=[pl.BlockSpec((tm, width), lambda i: (i, 0)),
                   pl.BlockSpec((1, width, tm), lambda i: (i // nl, 0, i % nl))],
        out_shape=[jax.ShapeDtypeStruct((nb * seq, width), BF16), jax.ShapeDtypeStruct((nb, width, seq), F32)],
        compiler_params=_cparams("parallel"),
        name="qk_prep_prompt",
    )(proj, proj, *_qk_consts(q_g, k_g))


def _qk_prep_sample(proj, q_g, k_g, nb, dec, width):
    q_rows = 2 * (width // LANES) * dec
    const = lambda i: (0, 0)
    return pl.pallas_call(
        _qk_prep_sample_kernel,
        grid=(nb,),
        in_specs=[pl.BlockSpec((dec, width), lambda i: (i, 0)), pl.BlockSpec((dec, width), lambda i: (i, 1)),
                  pl.BlockSpec((1, LANES), const), pl.BlockSpec((1, LANES), const), pl.BlockSpec((LANES, LANES), const)],
        out_specs=[pl.BlockSpec((q_rows, LANES), lambda i: (i, 0)), pl.BlockSpec((dec, width), lambda i: (i, 0))],
        out_shape=[jax.ShapeDtypeStruct((nb * q_rows, LANES), BF16), jax.ShapeDtypeStruct((nb * dec, width), F32)],
        compiler_params=_cparams("parallel"),
        name="qk_prep_sample",
    )(proj, proj, *_qk_consts(q_g, k_g))


def _diff_out(o1, o2, lam, lam_init, sub_g, gate):
    d = o1 - lam * o2
    d = d * lax.rsqrt(jnp.mean(d * d, axis=-1, keepdims=True) + EPS) * sub_g * (1.0 - lam_init)
    return d * _silu(gate)


def _prompt_attn_kernel(lam_ref, q_ref, kt_ref, v_ref, gate_ref, sg_ref, o_ref, ktb_ref, vb_ref, *, tq, lam_init):
    seq = q_ref.shape[0]
    lam = lam_ref[0]
    ktb_ref[...] = kt_ref[0].astype(BF16)
    vb_ref[:, 0:LANES] = v_ref[...].astype(BF16)
    vb_ref[:, LANES:2 * LANES] = jnp.ones((seq, LANES), BF16)
    lane = lax.broadcasted_iota(jnp.int32, (tq, LANES), 1)
    lo_half = lane < ATT_QK_DIM
    row = lax.broadcasted_iota(jnp.int32, (2 * tq, tq), 0)
    col = lax.broadcasted_iota(jnp.int32, (2 * tq, tq), 1)
    diag_ok = col <= jnp.where(row >= tq, row - tq, row)
    for qi in range(seq // tq):
        lo = qi * tq
        q = q_ref[lo:lo + tq, :]
        zero = jnp.zeros_like(q)
        q2 = jnp.concatenate([jnp.where(lo_half, q, zero), jnp.where(lo_half, zero, q)], axis=0)
        sd = jnp.where(diag_ok, _dot(q2, ktb_ref[:, lo:lo + tq]), NEG)
        m = jnp.max(sd, axis=-1, keepdims=True)
        if qi > 0:
            sp = _dot(q2, ktb_ref[:, 0:lo])
            m = jnp.maximum(m, jnp.max(sp, axis=-1, keepdims=True))
        acc = _dot(jnp.exp2(sd - m).astype(BF16), vb_ref[lo:lo + tq, :])
        if qi > 0:
            acc = acc + _dot(jnp.exp2(sp - m).astype(BF16), vb_ref[0:lo, :])
        o = acc[:, 0:LANES] / acc[:, LANES:2 * LANES]
        out = _diff_out(o[0:tq], o[tq:2 * tq], lam, lam_init, sg_ref[...], gate_ref[lo:lo + tq, :])
        o_ref[lo:lo + tq, :] = out.astype(o_ref.dtype)


def _prompt_attn(lam, qn, kt, proj, sub_g, nb, seq, heads, *, v_blk, gate_blk, lam_init, tq):
    m = nb * seq
    width = heads * ATT_V_DIM
    blk = lambda off: pl.BlockSpec((seq, LANES), lambda bi, h: (bi, off + h))
    return pl.pallas_call(
        functools.partial(_prompt_attn_kernel, tq=tq, lam_init=lam_init),
        grid=(nb, heads),
        in_specs=[pl.BlockSpec(memory_space=pltpu.SMEM), blk(0),
                  pl.BlockSpec((1, LANES, seq), lambda bi, h: (bi, h, 0)), blk(v_blk), blk(gate_blk),
                  pl.BlockSpec((1, LANES), lambda bi, h: (0, 0))],
        out_specs=blk(0),
        out_shape=jax.ShapeDtypeStruct((m, width), BF16),
        scratch_shapes=[pltpu.VMEM((LANES, seq), BF16), pltpu.VMEM((seq, 2 * LANES), BF16)],
        compiler_params=_cparams("parallel", "parallel"),
        name="prompt_attn",
    )(lam, qn, kt, proj, proj, sub_g.reshape(1, LANES))


def _sample_attn_kernel(pt_ref, lam_ref, q_ref, *rest, heads, dec, lam_init, group):
    kt_refs = rest[:group]
    v_refs = rest[group:2 * group]
    kn_ref, vn_ref, gate_ref, sg_ref, o_ref, m_ref, l_ref, acc_ref = rest[2 * group:]
    p = pl.program_id(1)
    rph = 2 * dec
    rows = heads * rph

    @pl.when(p == 0)
    def _():
        m_ref[...] = jnp.full(m_ref.shape, NEG, F32)
        l_ref[...] = jnp.zeros(l_ref.shape, F32)
        acc_ref[...] = jnp.zeros(acc_ref.shape, F32)

    def q_head(h):
        return q_ref[h * rph:(h + 1) * rph, :]

    def update(s, v_head, mask):
        if mask is not None:
            s = jnp.where(mask, s, NEG)
        m_new = jnp.maximum(m_ref[...], jnp.max(s, axis=-1, keepdims=True))
        alpha = jnp.exp2(m_ref[...] - m_new)
        pr = jnp.exp2(s - m_new)
        l_ref[...] = alpha * l_ref[...] + jnp.sum(pr, axis=-1, keepdims=True)
        prb = pr.astype(BF16)
        pv = [_dot(prb[h * rph:(h + 1) * rph, :], v_head(h)) for h in range(heads)]
        acc_ref[...] = alpha * acc_ref[...] + jnp.concatenate(pv, axis=0)
        m_ref[...] = m_new

    def kt_head(h):
        return jnp.concatenate([r[0, h * LANES:(h + 1) * LANES, :] for r in kt_refs], axis=1).astype(BF16)

    def v_head(h):
        return jnp.concatenate([r[0, pl.ds(h, PAGE, stride=heads), :] for r in v_refs], axis=0).astype(BF16)

    update(jnp.concatenate([_dot(q_head(h), kt_head(h)) for h in range(heads)], axis=0), v_head, None)

    @pl.when(p == pl.num_programs(1) - 1)
    def _():
        zpad = jnp.zeros((PAGE - dec, LANES), F32)

        def new_head(ref, h):
            return jnp.concatenate([ref[:, h * LANES:(h + 1) * LANES], zpad], axis=0).astype(BF16)

        s = jnp.concatenate([_dot_nt(q_head(h), new_head(kn_ref, h)) for h in range(heads)], axis=0)
        row = lax.broadcasted_iota(jnp.int32, (rows, PAGE), 0)
        col = lax.broadcasted_iota(jnp.int32, (rows, PAGE), 1)
        update(s, functools.partial(new_head, vn_ref), col <= row % dec)
        o = acc_ref[...] / l_ref[...]
        lam = lam_ref[0]
        outs = []
        for h in range(heads):
            o1 = o[h * rph:h * rph + dec]
            o2 = o[h * rph + dec:(h + 1) * rph]
            outs.append(_diff_out(o1, o2, lam, lam_init, sg_ref[...], gate_ref[:, h * ATT_V_DIM:(h + 1) * ATT_V_DIM]))
        o_ref[...] = jnp.concatenate(outs, axis=1)


def _sample_attn(page_table, lam, qs, cache_kt, cache_v, kn, proj, sub_g, nb, dec, heads, *, page0, v_blk, gate_blk,
                 lam_init):
    width = heads * ATT_V_DIM
    rows = heads * 2 * dec
    n_pages = page_table.shape[1]
    group = math.gcd(n_pages, PAGE_GROUP)

    def page(i):
        return lambda bi, p, pt: (page0 + pt[bi, group * p + i], 0, 0)

    grid_spec = pltpu.PrefetchScalarGridSpec(
        num_scalar_prefetch=1,
        grid=(nb, n_pages // group),
        in_specs=([pl.BlockSpec(memory_space=pltpu.SMEM),
                   pl.BlockSpec((rows, LANES), lambda bi, p, pt: (bi, 0))]
                  + [pl.BlockSpec((1, width, PAGE), page(i)) for i in range(group)]
                  + [pl.BlockSpec((1, PAGE * heads, LANES), page(i)) for i in range(group)]
                  + [pl.BlockSpec((dec, width), lambda bi, p, pt: (bi, 0)),
                     pl.BlockSpec((dec, width), lambda bi, p, pt: (bi, v_blk)),
                     pl.BlockSpec((dec, width), lambda bi, p, pt: (bi, gate_blk)),
                     pl.BlockSpec((1, LANES), lambda bi, p, pt: (0, 0))]),
        out_specs=pl.BlockSpec((dec, width), lambda bi, p, pt: (bi, 0)),
        scratch_shapes=[pltpu.VMEM((rows, 1), F32), pltpu.VMEM((rows, 1), F32), pltpu.VMEM((rows, ATT_V_DIM), F32)],
    )
    return pl.pallas_call(
        functools.partial(_sample_attn_kernel, heads=heads, dec=dec, lam_init=lam_init, group=group),
        grid_spec=grid_spec,
        out_shape=jax.ShapeDtypeStruct((nb * dec, width), F32),
        compiler_params=_cparams("parallel", "arbitrary"),
        name="sample_attn",
    )(page_table, lam, qs, *([cache_kt] * group), *([cache_v] * group), kn, proj, proj, sub_g.reshape(1, LANES))


def _even_layer(x, nb, seq, conv_prev, ssd_conv_prev, ssd_prev, p, *, tm, tl, q, act_dtype):
    ch = p["conv_w"].shape[1]
    proj, dtp = _norm_matmul(x, p["norm_g"], p["w_main"], p["w_dt"], tm=tm, tn=1024)
    if conv_prev is not None:
        conv_prev = jnp.pad(conv_prev, ((0, 0), (CONV_PAD - (CONV_WIDTH - 1), 0), (0, 0)))
        ssd_conv_prev = jnp.pad(ssd_conv_prev, ((0, 0), (SSD_PAD - (SSD_CONV - 1), 0), (0, 0)))
        ssd_prev = ssd_prev.reshape(nb, -1, SSD_STATE)
    ya, conv_new = _conv_a(proj, nb, seq, conv_prev, p["conv_w"], p["conv_b"], p["ln_g"], p["ln_b"],
                           tl=tl, out_dtype=act_dtype)
    yb, ssd_conv_new, ssd_new = _ssd(proj, dtp, nb, seq, ssd_conv_prev, ssd_prev, p["ssd_conv_w"], p["ssd_conv_b"],
                                     p["dt_bias"], p["a_log"], p["d_skip"], p["ssd_norm_g"],
                                     col0=3 * ch, q=q, out_dtype=act_dtype)
    y = _matmul_res([ya, yb], [p["w_out_a"], p["w_out_b"]], x, tm=tm, tn=512)
    return y, conv_new, ssd_conv_new, ssd_new


def kernel(x_prompt, x_sample, state_conv_a, state_ssd_conv, state_ssd, cache_k, cache_v, page_table, norm_even, w_in_even, conv_a_w, conv_a_b, conv_a_ln_g, conv_a_ln_b, ssd_conv_w, ssd_conv_b, ssd_dt_bias, ssd_a_log, ssd_d, ssd_norm_g, w_out_even, norm_odd, w_in_odd, q_norm_g, k_norm_g, lam_q1, lam_k1, lam_q2, lam_k2, subln_g, w_out_odd):
    bp, seq, d = x_prompt.shape
    bs, dec, _ = x_sample.shape
    depth = norm_even.shape[0] + norm_odd.shape[0]
    xp = x_prompt.reshape(bp * seq, d)
    xs = x_sample.reshape(bs * dec, d)
    ch = conv_a_w.shape[2]
    ssd_heads = ssd_dt_bias.shape[1]
    inner = ssd_norm_g.shape[1]
    att_width = w_out_odd.shape[1]
    heads = att_width // ATT_V_DIM
    main_cols = w_in_even.shape[2] - ssd_heads

    tm_p = min(1024, bp * seq)
    tq = min(256, seq)

    ca_p, ca_s, sc_p, sc_s, ss_p, ss_s = [], [], [], [], [], []
    k_p, k_s, v_p, v_s = [], [], [], []
    for layer in range(depth):
        j = layer // 2
        if layer % 2 == 0:
            w_in = w_in_even[j]
            p = dict(norm_g=norm_even[j],
                     w_main=w_in[:, :main_cols].astype(BF16),
                     w_dt=jnp.pad(w_in[:, main_cols:], ((0, 0), (0, LANES - ssd_heads))).astype(BF16),
                     conv_w=conv_a_w[j], conv_b=conv_a_b[j], ln_g=conv_a_ln_g[j], ln_b=conv_a_ln_b[j],
                     ssd_conv_w=ssd_conv_w[j], ssd_conv_b=ssd_conv_b[j], dt_bias=ssd_dt_bias[j], a_log=ssd_a_log[j],
                     d_skip=ssd_d[j], ssd_norm_g=ssd_norm_g[j],
                     w_out_a=w_out_even[j][:ch].astype(BF16), w_out_b=w_out_even[j][ch:].astype(BF16))
            xp, c1, c2, c3 = _even_layer(xp, bp, seq, None, None, None, p, tm=tm_p, tl=256, q=SSD_CHUNK, act_dtype=BF16)
            xs, d1, d2, d3 = _even_layer(xs, bs, dec, state_conv_a[j], state_ssd_conv[j], state_ssd[j], p,
                                         tm=bs * dec, tl=dec, q=dec, act_dtype=F32)
            ca_p.append(c1)
            sc_p.append(c2)
            ss_p.append(c3.reshape(bp, ssd_heads, SSD_HEAD_DIM, SSD_STATE))
            ca_s.append(d1)
            sc_s.append(d2)
            ss_s.append(d3.reshape(bs, ssd_heads, SSD_HEAD_DIM, SSD_STATE))
        else:
            lam_init = 0.8 - 0.6 * math.exp(-0.3 * layer)
            lam = (jnp.exp(jnp.sum(lam_q1[j].astype(F32) * lam_k1[j].astype(F32)))
                   - jnp.exp(jnp.sum(lam_q2[j].astype(F32) * lam_k2[j].astype(F32))) + lam_init).reshape(1)
            w_in = w_in_odd[j].astype(BF16)
            w_out = w_out_odd[j].astype(BF16)
            v_blk, gate_blk = 2 * att_width // LANES, 3 * att_width // LANES

            proj = _norm_matmul(xp, norm_odd[j], w_in, tm=tm_p, tn=1024)
            qn, kt = _qk_prep_prompt(proj, q_norm_g[j], k_norm_g[j], bp, seq, att_width, tm=min(512, seq))
            o = _prompt_attn(lam, qn, kt, proj, subln_g[j], bp, seq, heads,
                             v_blk=v_blk, gate_blk=gate_blk, lam_init=lam_init, tq=tq)
            k_p.append(jnp.transpose(kt.reshape(bp, heads, 2, ATT_QK_DIM, seq), (0, 4, 1, 2, 3)))
            v_p.append(proj[:, 2 * att_width:3 * att_width].reshape(bp, seq, heads, ATT_V_DIM))
            xp = _matmul_res([o], [w_out], xp, tm=tm_p, tn=512)

            proj = _norm_matmul(xs, norm_odd[j], w_in, tm=bs * dec, tn=1024)
            qs, kn = _qk_prep_sample(proj, q_norm_g[j], k_norm_g[j], bs, dec, att_width)
            n_pool = cache_k.shape[1]
            cache_kt = jnp.transpose(cache_k, (0, 1, 3, 4, 5, 2)).reshape(-1, att_width, PAGE)
            o = _sample_attn(page_table, lam, qs, cache_kt, cache_v.reshape(-1, PAGE * heads, ATT_V_DIM), kn, proj,
                             subln_g[j], bs, dec, heads, page0=j * n_pool, v_blk=2, gate_blk=3, lam_init=lam_init)
            k_s.append(kn.reshape(bs, dec, heads, 2, ATT_QK_DIM))
            v_s.append(proj[:, 2 * att_width:3 * att_width].reshape(bs, dec, heads, ATT_V_DIM))
            xs = _matmul_res([o], [w_out], xs, tm=bs * dec, tn=512)
    return (xp.reshape(bp, seq, d), xs.reshape(bs, dec, d), jnp.stack(ca_p), jnp.stack(ca_s), jnp.stack(sc_p),
            jnp.stack(sc_s), jnp.stack(ss_p), jnp.stack(ss_s), jnp.stack(k_p), jnp.stack(k_s), jnp.stack(v_p),
            jnp.stack(v_s))
```

```python
import functools
import math

import jax
import jax.numpy as jnp
from jax import lax
from jax.experimental import pallas as pl
from jax.experimental.pallas import tpu as pltpu

F32 = jnp.float32
BF16 = jnp.bfloat16

EPS = 1e-6
NEG = -1e30
LANES = 128
SUBLANES = 8
CONV_WIDTH = 31
CONV_PAD = 32
SSD_HEAD_DIM = 64
SSD_GROUPS = 4
SSD_STATE = 128
SSD_CONV = 4
SSD_PAD = 8
SSD_CHUNK = 128
ATT_QK_DIM = 64
ATT_V_DIM = 128
PAGE = 128
PAGE_GROUP = 4
VMEM_LIMIT = 56 * 1024 * 1024


def _cparams(*sem):
    return pltpu.CompilerParams(dimension_semantics=sem, vmem_limit_bytes=VMEM_LIMIT)


def _sigmoid(x):
    return 0.5 * jnp.tanh(0.5 * x) + 0.5


def _silu(x):
    return x * _sigmoid(x)


def _split2(x):
    hi = x.astype(BF16)
    lo = (x - hi.astype(F32)).astype(BF16)
    return hi, lo


def _split3(x):
    hi = x.astype(BF16)
    r = x - hi.astype(F32)
    mid = r.astype(BF16)
    lo = (r - mid.astype(F32)).astype(BF16)
    return hi, mid, lo


def _dot(a, b):
    return jnp.dot(a, b, preferred_element_type=F32)


def _dot_nt(a, b):
    return lax.dot_general(a, b, (((1,), (1,)), ((), ())), preferred_element_type=F32)


def _dot_tn(a, b):
    return lax.dot_general(a, b, (((0,), (0,)), ((), ())), preferred_element_type=F32)


def _norm_matmul_kernel(x_ref, g_ref, w_ref, *rest, has_aux):
    if has_aux:
        wa_ref, o_ref, oa_ref, hn_ref = rest
    else:
        o_ref, hn_ref = rest

    @pl.when(pl.program_id(1) == 0)
    def _():
        x = x_ref[...]
        ms = jnp.mean(x * x, axis=-1, keepdims=True)
        hn_ref[...] = (x * lax.rsqrt(ms + EPS) * g_ref[...]).astype(BF16)
        if has_aux:
            oa_ref[...] = _dot(hn_ref[...], wa_ref[...])

    o_ref[...] = _dot(hn_ref[...], w_ref[...])


def _norm_matmul(x, g, w, w_aux=None, *, tm, tn):
    m, k = x.shape
    n = w.shape[1]
    assert m % tm == 0 and n % tn == 0
    has_aux = w_aux is not None
    in_specs = [pl.BlockSpec((tm, k), lambda i, j: (i, 0)),
                pl.BlockSpec((1, k), lambda i, j: (0, 0)),
                pl.BlockSpec((k, tn), lambda i, j: (0, j))]
    out_specs = [pl.BlockSpec((tm, tn), lambda i, j: (i, j))]
    out_shape = [jax.ShapeDtypeStruct((m, n), F32)]
    args = [x, g.reshape(1, k), w]
    if has_aux:
        na = w_aux.shape[1]
        in_specs.append(pl.BlockSpec((k, na), lambda i, j: (0, 0)))
        out_specs.append(pl.BlockSpec((tm, na), lambda i, j: (i, 0)))
        out_shape.append(jax.ShapeDtypeStruct((m, na), F32))
        args.append(w_aux)
    outs = pl.pallas_call(
        functools.partial(_norm_matmul_kernel, has_aux=has_aux),
        grid=(m // tm, n // tn),
        in_specs=in_specs, out_specs=out_specs, out_shape=out_shape,
        scratch_shapes=[pltpu.VMEM((tm, k), BF16)],
        compiler_params=_cparams("parallel", "arbitrary"),
        name="norm_matmul",
    )(*args)
    return outs if has_aux else outs[0]


def _matmul_res_kernel(*refs, n_parts):
    a_refs = refs[:n_parts]
    w_refs = refs[n_parts:2 * n_parts]
    res_ref, o_ref = refs[2 * n_parts], refs[2 * n_parts + 1]
    acc = res_ref[...]
    for a_ref, w_ref in zip(a_refs, w_refs):
        acc = acc + _dot(a_ref[...].astype(BF16), w_ref[...])
    o_ref[...] = acc


def _matmul_res(a_parts, w_parts, res, *, tm, tn):
    m, n = res.shape
    assert m % tm == 0 and n % tn == 0
    n_parts = len(a_parts)
    in_specs = ([pl.BlockSpec((tm, a.shape[1]), lambda i, j: (i, 0)) for a in a_parts]
                + [pl.BlockSpec((w.shape[0], tn), lambda i, j: (0, j)) for w in w_parts]
                + [pl.BlockSpec((tm, tn), lambda i, j: (i, j))])
    return pl.pallas_call(
        functools.partial(_matmul_res_kernel, n_parts=n_parts),
        grid=(m // tm, n // tn),
        in_specs=in_specs,
        out_specs=pl.BlockSpec((tm, tn), lambda i, j: (i, j)),
        out_shape=jax.ShapeDtypeStruct((m, n), F32),
        compiler_params=_cparams("parallel", "arbitrary"),
        name="matmul_res",
    )(*a_parts, *w_parts, res)


def _conv_a_kernel(val_ref, glu_ref, gate_ref, *rest, tl, has_prefix, row_blk, ln_blk, ln_par):
    if has_prefix:
        pre_ref, w_ref, b_ref, lg_ref, lb_ref, ya_ref, st_ref, ubuf, cbuf = rest
    else:
        w_ref, b_ref, lg_ref, lb_ref, ya_ref, st_ref, ubuf, cbuf = rest
    ch = val_ref.shape[1]
    nlb = ch // LANES
    off = CONV_PAD - (CONV_WIDTH - 1)
    lanes = [slice(c * LANES, (c + 1) * LANES) for c in range(nlb)]

    @pl.when(pl.program_id(1) == 0)
    def _():
        for c in range(nlb):
            ubuf[c, 0:CONV_PAD, :] = pre_ref[0, :, lanes[c]] if has_prefix else jnp.zeros((CONV_PAD, LANES), F32)

    for c in range(nlb):
        ubuf[c, CONV_PAD:CONV_PAD + tl, :] = val_ref[:, lanes[c]] * _sigmoid(glu_ref[:, lanes[c]])

    def lane_body(c, carry):
        c0 = pl.multiple_of(c * LANES, LANES)
        taps = [w_ref[k:k + 1, pl.ds(c0, LANES)] for k in range(CONV_WIDTH)]
        bias = b_ref[:, pl.ds(c0, LANES)]
        for r0 in range(0, tl, row_blk):
            acc = jnp.zeros((row_blk, LANES), F32) + bias
            for sh in range(SUBLANES):
                ks = list(range(sh, CONV_WIDTH, SUBLANES))
                span = row_blk + ks[-1] - sh
                win = ubuf[c, r0 + off + sh:r0 + off + sh + span, :]
                for k in ks:
                    acc = acc + win[k - sh:k - sh + row_blk] * taps[k]
            cbuf[c, r0:r0 + row_blk, :] = acc
        return carry

    lax.fori_loop(0, nlb, lane_body, 0)

    for c in range(nlb):
        tail = ubuf[c, tl:tl + CONV_PAD, :]
        st_ref[0, :, lanes[c]] = tail
        ubuf[c, 0:CONV_PAD, :] = tail

    def ln_body(r, carry):
        for j in range(ln_par):
            r0 = pl.multiple_of((r * ln_par + j) * ln_blk, ln_blk)
            u = jnp.concatenate([cbuf[c, pl.ds(r0, ln_blk), :] for c in range(nlb)], axis=1)
            mu = jnp.mean(u, axis=-1, keepdims=True)
            uc = u - mu
            var = jnp.mean(uc * uc, axis=-1, keepdims=True)
            y = uc * lax.rsqrt(var + EPS) * lg_ref[...] + lb_ref[...]
            ya = _silu(y) * _silu(gate_ref[pl.ds(r0, ln_blk), :])
            ya_ref[pl.ds(r0, ln_blk), :] = ya.astype(ya_ref.dtype)
        return carry

    lax.fori_loop(0, tl // (ln_blk * ln_par), ln_body, 0)


def _conv_a(proj, nb, seq, prefix, w, b, lg, lb, *, tl, out_dtype):
    ch = w.shape[1]
    nl = seq // tl
    has_prefix = prefix is not None
    row_blk = min(tl, 64)
    ln_blk = min(tl, 16)
    ln_par = max(1, min(4, tl // ln_blk))
    in_specs = [pl.BlockSpec((tl, ch), lambda bi, i: (bi * nl + i, 0)),
                pl.BlockSpec((tl, ch), lambda bi, i: (bi * nl + i, 1)),
                pl.BlockSpec((tl, ch), lambda bi, i: (bi * nl + i, 2))]
    args = [proj, proj, proj]
    if has_prefix:
        in_specs.append(pl.BlockSpec((1, CONV_PAD, ch), lambda bi, i: (bi, 0, 0)))
        args.append(prefix)
    in_specs += [pl.BlockSpec((CONV_WIDTH, ch), lambda bi, i: (0, 0))] + [pl.BlockSpec((1, ch), lambda bi, i: (0, 0))] * 3
    args += [w, b.reshape(1, ch), lg.reshape(1, ch), lb.reshape(1, ch)]
    ya, st = pl.pallas_call(
        functools.partial(_conv_a_kernel, tl=tl, has_prefix=has_prefix, row_blk=row_blk, ln_blk=ln_blk, ln_par=ln_par),
        grid=(nb, nl),
        in_specs=in_specs,
        out_specs=[pl.BlockSpec((tl, ch), lambda bi, i: (bi * nl + i, 0)),
                   pl.BlockSpec((1, CONV_PAD, ch), lambda bi, i: (bi, 0, 0))],
        out_shape=[jax.ShapeDtypeStruct((nb * seq, ch), out_dtype),
                   jax.ShapeDtypeStruct((nb, CONV_PAD, ch), F32)],
        scratch_shapes=[pltpu.VMEM((ch // LANES, CONV_PAD + tl, LANES), F32), pltpu.VMEM((ch // LANES, tl, LANES), F32)],
        compiler_params=_cparams("parallel", "arbitrary"),
        name="conv_a",
    )(*args)
    return ya, st[:, CONV_PAD - (CONV_WIDTH - 1):, :]


def _ssd_kernel(z_ref, xs_ref, bc_ref, dt_ref, *rest, q, qp, has_state):
    if has_state:
        pre_ref, h0_ref = rest[:2]
        rest = rest[2:]
    (cw_ref, cb_ref, dtb_ref, alog_ref, dsk_ref, ng_ref, e_ref, et_ref,
     yb_ref, cst_ref, hst_ref, xbuf, h_ref) = rest
    inner = xs_ref.shape[1]
    nbc = bc_ref.shape[1]
    xbc_w = inner + nbc
    gw = inner // SSD_GROUPS
    hpg = gw // SSD_HEAD_DIM
    off = SSD_PAD - (SSD_CONV - 1)

    nlb = xbc_w // LANES
    lanes = [slice(c * LANES, (c + 1) * LANES) for c in range(nlb)]

    @pl.when(pl.program_id(1) == 0)
    def _():
        for c in range(nlb):
            xbuf[c, 0:SSD_PAD, :] = pre_ref[0, :, lanes[c]] if has_state else jnp.zeros((SSD_PAD, LANES), F32)
        h_ref[...] = h0_ref[0] if has_state else jnp.zeros(h_ref.shape, F32)

    conv_parts = []
    for c in range(nlb):
        src = xs_ref[:, lanes[c]] if c < inner // LANES else bc_ref[:, c * LANES - inner:(c + 1) * LANES - inner]
        xbuf[c, SSD_PAD:SSD_PAD + q, :] = src
        acc = jnp.zeros((q, LANES), F32) + cb_ref[:, lanes[c]]
        for k in range(SSD_CONV):
            acc = acc + xbuf[c, off + k:off + k + q, :] * cw_ref[k:k + 1, lanes[c]]
        conv_parts.append(_silu(acc))
        tail = xbuf[c, q:q + SSD_PAD, :]
        cst_ref[0, :, lanes[c]] = tail
        xbuf[c, 0:SSD_PAD, :] = tail
    xbc = jnp.concatenate(conv_parts, axis=1)

    dt_raw = dt_ref[...] + dtb_ref[...]
    dt = jnp.maximum(dt_raw, 0.0) + jnp.log(1.0 + jnp.exp(-jnp.abs(dt_raw)))
    zg = z_ref[...]
    if q < qp:
        xbc = jnp.concatenate([xbc, jnp.zeros((qp - q, xbc_w), F32)], axis=0)
        dt = jnp.concatenate([dt, jnp.zeros((qp - q, LANES), F32)], axis=0)
        zg = jnp.concatenate([zg, jnp.zeros((qp - q, inner), F32)], axis=0)
    x = xbc[:, 0:inner]

    a_neg = -jnp.exp(alog_ref[...])
    dta = dt * a_neg
    row = lax.broadcasted_iota(jnp.int32, (qp, qp), 0)
    col = lax.broadcasted_iota(jnp.int32, (qp, qp), 1)
    causal = row >= col
    tril = jnp.where(causal, 1.0, 0.0).astype(BF16)
    acs = sum(_dot(tril, part) for part in _split3(dta))
    acs_t = acs.T
    last = acs[qp - 1:qp, :]
    eat = jnp.exp(acs)
    ddt = jnp.exp(last - acs) * dt

    stack = jnp.concatenate([dt, ddt, eat], axis=0)
    ex = sum(_dot(part, e_ref[...]) for part in _split2(stack))
    xdt = (x * ex[0:qp]).astype(BF16)
    xdd = (x * ex[qp:2 * qp]).astype(BF16)
    eat_x = ex[2 * qp:3 * qp]

    lane = lax.broadcasted_iota(jnp.int32, (qp, LANES), 1)
    lo_half = lane < SSD_HEAD_DIM
    y_parts, s_parts = [], []
    for g in range(SSD_GROUPS):
        bg = xbc[:, inner + g * SSD_STATE:inner + (g + 1) * SSD_STATE].astype(BF16)
        cg = xbc[:, inner + (SSD_GROUPS + g) * SSD_STATE:inner + (SSD_GROUPS + g + 1) * SSD_STATE].astype(BF16)
        cb = _dot_nt(cg, bg)
        hg = h_ref[g * gw:(g + 1) * gw, :].astype(BF16)
        y_off = _dot_nt(cg, hg) * eat_x[:, g * gw:(g + 1) * gw]
        s_parts.append(_dot_tn(xdd[:, g * gw:(g + 1) * gw], bg))
        yd = []
        for j in range(hpg // 2):
            h0 = g * hpg + 2 * j
            ms = []
            for h in (h0, h0 + 1):
                seg = jnp.where(causal, acs[:, h:h + 1] - acs_t[h:h + 1, :], -jnp.inf)
                ms.append(cb * jnp.exp(seg))
            lhs = jnp.concatenate(ms, axis=1).astype(BF16)
            xp = xdt[:, h0 * SSD_HEAD_DIM:h0 * SSD_HEAD_DIM + LANES]
            zero = jnp.zeros_like(xp)
            rhs = jnp.concatenate([jnp.where(lo_half, xp, zero), jnp.where(lo_half, zero, xp)], axis=0)
            yd.append(_dot(lhs, rhs))
        y_parts.append(jnp.concatenate(yd, axis=1) + y_off)
    y = jnp.concatenate(y_parts, axis=1) + dsk_ref[...] * x

    dec = jnp.broadcast_to(jnp.exp(acs_t[:, qp - 1:qp]), (LANES, LANES))
    dec_rows = sum(_dot(et_ref[...], part) for part in _split2(dec))
    h_new = h_ref[...] * dec_rows + jnp.concatenate(s_parts, axis=0)
    h_ref[...] = h_new
    hst_ref[0] = h_new

    y = y * _silu(zg)
    outs = []
    for g in range(SSD_GROUPS):
        yg = y[:, g * gw:(g + 1) * gw]
        outs.append(yg * lax.rsqrt(jnp.mean(yg * yg, axis=-1, keepdims=True) + EPS))
    yn = jnp.concatenate(outs, axis=1) * ng_ref[...]
    yb_ref[...] = yn[0:q].astype(yb_ref.dtype)


def _ssd(proj, dtp, nb, seq, prefix, h0, cw, cb, dt_bias, a_log, d_skip, norm_g, *, col0, q, out_dtype):
    inner = norm_g.shape[0]
    heads = dt_bias.shape[0]
    nbc = 2 * SSD_GROUPS * SSD_STATE
    xbc_w = inner + nbc
    nc = seq // q
    qp = SSD_CHUNK
    has_state = h0 is not None
    zb, xb, bcb = col0 // inner, col0 // inner + 1, (col0 + 2 * inner) // nbc
    assert col0 % inner == 0 and (col0 + 2 * inner) % nbc == 0

    lane_head = jnp.arange(inner, dtype=jnp.int32) // SSD_HEAD_DIM
    expand = (jnp.arange(LANES, dtype=jnp.int32)[:, None] == lane_head[None, :]).astype(BF16)
    pad = LANES - heads
    dtb = jnp.pad(dt_bias.astype(F32), (0, pad)).reshape(1, LANES)
    alog = jnp.pad(a_log.astype(F32), (0, pad)).reshape(1, LANES)
    dsk = jnp.repeat(d_skip.astype(F32), SSD_HEAD_DIM).reshape(1, inner)

    in_specs = [pl.BlockSpec((q, inner), lambda bi, c: (bi * nc + c, zb)),
                pl.BlockSpec((q, inner), lambda bi, c: (bi * nc + c, xb)),
                pl.BlockSpec((q, nbc), lambda bi, c: (bi * nc + c, bcb)),
                pl.BlockSpec((q, LANES), lambda bi, c: (bi * nc + c, 0))]
    args = [proj, proj, proj, dtp]
    if has_state:
        in_specs += [pl.BlockSpec((1, SSD_PAD, xbc_w), lambda bi, c: (bi, 0, 0)),
                     pl.BlockSpec((1, inner, SSD_STATE), lambda bi, c: (bi, 0, 0))]
        args += [prefix, h0]
    const = lambda bi, c: (0, 0)
    in_specs += [pl.BlockSpec((SSD_CONV, xbc_w), const), pl.BlockSpec((1, xbc_w), const),
                 pl.BlockSpec((1, LANES), const), pl.BlockSpec((1, LANES), const),
                 pl.BlockSpec((1, inner), const), pl.BlockSpec((1, inner), const),
                 pl.BlockSpec((LANES, inner), const), pl.BlockSpec((inner, LANES), const)]
    args += [cw, cb.reshape(1, xbc_w), dtb, alog, dsk, norm_g.reshape(1, inner), expand, expand.T]
    yb, cst, hst = pl.pallas_call(
        functools.partial(_ssd_kernel, q=q, qp=qp, has_state=has_state),
        grid=(nb, nc),
        in_specs=in_specs,
        out_specs=[pl.BlockSpec((q, inner), lambda bi, c: (bi * nc + c, 0)),
                   pl.BlockSpec((1, SSD_PAD, xbc_w), lambda bi, c: (bi, 0, 0)),
                   pl.BlockSpec((1, inner, SSD_STATE), lambda bi, c: (bi, 0, 0))],
        out_shape=[jax.ShapeDtypeStruct((nb * seq, inner), out_dtype),
                   jax.ShapeDtypeStruct((nb, SSD_PAD, xbc_w), F32),
                   jax.ShapeDtypeStruct((nb, inner, SSD_STATE), F32)],
        scratch_shapes=[pltpu.VMEM((xbc_w // LANES, SSD_PAD + q, LANES), F32), pltpu.VMEM((inner, SSD_STATE), F32)],
        compiler_params=_cparams("parallel", "arbitrary"),
        name="ssd",
    )(*args)
    return yb, cst[:, SSD_PAD - (SSD_CONV - 1):, :], hst


def _seg_rms(xb, gain, bd):
    ss = sum(_dot(part, bd) for part in _split2(xb * xb))
    return xb * lax.rsqrt(ss * (1.0 / ATT_QK_DIM) + EPS) * gain


Q_SCALE = ATT_QK_DIM ** -0.5 * math.log2(math.e)


def _qk_prep_prompt_kernel(q_ref, k_ref, qg_ref, kg_ref, bd_ref, qo_ref, kt_ref):
    width = q_ref.shape[1]
    bd = bd_ref[...]
    for c in range(width // LANES):
        cs = slice(c * LANES, (c + 1) * LANES)
        kt_ref[0, cs, :] = _seg_rms(k_ref[:, cs], kg_ref[...], bd).T
        qo_ref[:, cs] = (_seg_rms(q_ref[:, cs], qg_ref[...], bd) * Q_SCALE).astype(qo_ref.dtype)


def _qk_prep_sample_kernel(q_ref, k_ref, qg_ref, kg_ref, bd_ref, qo_ref, ko_ref):
    rows, width = q_ref.shape
    bd = bd_ref[...]
    lo_half = lax.broadcasted_iota(jnp.int32, (rows, LANES), 1) < ATT_QK_DIM
    blocks = []
    for c in range(width // LANES):
        cs = slice(c * LANES, (c + 1) * LANES)
        ko_ref[:, cs] = _seg_rms(k_ref[:, cs], kg_ref[...], bd)
        qn = _seg_rms(q_ref[:, cs], qg_ref[...], bd) * Q_SCALE
        blocks += [jnp.where(lo_half, qn, 0.0), jnp.where(lo_half, 0.0, qn)]
    qo_ref[...] = jnp.concatenate(blocks, axis=0).astype(qo_ref.dtype)


def _qk_consts(q_g, k_g):
    seg = jnp.arange(LANES, dtype=jnp.int32) // ATT_QK_DIM
    bd = (seg[:, None] == seg[None, :]).astype(BF16)
    qg = jnp.tile(q_g.astype(F32), LANES // ATT_QK_DIM).reshape(1, LANES)
    kg = jnp.tile(k_g.astype(F32), LANES // ATT_QK_DIM).reshape(1, LANES)
    return qg, kg, bd


def _qk_prep_prompt(proj, q_g, k_g, nb, seq, width, *, tm):
    nl = seq // tm
    const = lambda i: (0, 0)
    return pl.pallas_call(
        _qk_prep_prompt_kernel,
        grid=(nb * nl,),
        in_specs=[pl.BlockSpec((tm, width), lambda i: (i, 0)), pl.BlockSpec((tm, width), lambda i: (i, 1)),
                  pl.BlockSpec((1, LANES), const), pl.BlockSpec((1, LANES), const), pl.BlockSpec((LANES, LANES), const)],
        out_specs=[pl.BlockSpec((tm, width), lambda i: (i, 0)),
                   pl.BlockSpec((1, width, tm), lambda i: (i // nl, 0, i % nl))],
        out_shape=[jax.ShapeDtypeStruct((nb * seq, width), BF16), jax.ShapeDtypeStruct((nb, width, seq), F32)],
        compiler_params=_cparams("parallel"),
        name="qk_prep_prompt",
    )(proj, proj, *_qk_consts(q_g, k_g))


def _qk_prep_sample(proj, q_g, k_g, nb, dec, width):
    q_rows = 2 * (width // LANES) * dec
    const = lambda i: (0, 0)
    return pl.pallas_call(
        _qk_prep_sample_kernel,
        grid=(nb,),
        in_specs=[pl.BlockSpec((dec, width), lambda i: (i, 0)), pl.BlockSpec((dec, width), lambda i: (i, 1)),
                  pl.BlockSpec((1, LANES), const), pl.BlockSpec((1, LANES), const), pl.BlockSpec((LANES, LANES), const)],
        out_specs=[pl.BlockSpec((q_rows, LANES), lambda i: (i, 0)), pl.BlockSpec((dec, width), lambda i: (i, 0))],
        out_shape=[jax.ShapeDtypeStruct((nb * q_rows, LANES), BF16), jax.ShapeDtypeStruct((nb * dec, width), F32)],
        compiler_params=_cparams("parallel"),
        name="qk_prep_sample",
    )(proj, proj, *_qk_consts(q_g, k_g))


def _diff_out(o1, o2, lam, lam_init, sub_g, gate):
    d = o1 - lam * o2
    d = d * lax.rsqrt(jnp.mean(d * d, axis=-1, keepdims=True) + EPS) * sub_g * (1.0 - lam_init)
    return d * _silu(gate)


def _prompt_attn_kernel(lam_ref, q_ref, kt_ref, v_ref, gate_ref, sg_ref, o_ref, ktb_ref, vb_ref, *, tq, lam_init):
    seq = q_ref.shape[0]
    lam = lam_ref[0]
    ktb_ref[...] = kt_ref[0].astype(BF16)
    vb_ref[:, 0:LANES] = v_ref[...].astype(BF16)
    vb_ref[:, LANES:2 * LANES] = jnp.ones((seq, LANES), BF16)
    lane = lax.broadcasted_iota(jnp.int32, (tq, LANES), 1)
    lo_half = lane < ATT_QK_DIM
    row = lax.broadcasted_iota(jnp.int32, (2 * tq, tq), 0)
    col = lax.broadcasted_iota(jnp.int32, (2 * tq, tq), 1)
    diag_ok = col <= jnp.where(row >= tq, row - tq, row)
    for qi in range(seq // tq):
        lo = qi * tq
        q = q_ref[lo:lo + tq, :]
        zero = jnp.zeros_like(q)
        q2 = jnp.concatenate([jnp.where(lo_half, q, zero), jnp.where(lo_half, zero, q)], axis=0)
        sd = jnp.where(diag_ok, _dot(q2, ktb_ref[:, lo:lo + tq]), NEG)
        m = jnp.max(sd, axis=-1, keepdims=True)
        if qi > 0:
            sp = _dot(q2, ktb_ref[:, 0:lo])
            m = jnp.maximum(m, jnp.max(sp, axis=-1, keepdims=True))
        acc = _dot(jnp.exp2(sd - m).astype(BF16), vb_ref[lo:lo + tq, :])
        if qi > 0:
            acc = acc + _dot(jnp.exp2(sp - m).astype(BF16), vb_ref[0:lo, :])
        o = acc[:, 0:LANES] / acc[:, LANES:2 * LANES]
        out = _diff_out(o[0:tq], o[tq:2 * tq], lam, lam_init, sg_ref[...], gate_ref[lo:lo + tq, :])
        o_ref[lo:lo + tq, :] = out.astype(o_ref.dtype)


def _prompt_attn(lam, qn, kt, proj, sub_g, nb, seq, heads, *, v_blk, gate_blk, lam_init, tq):
    m = nb * seq
    width = heads * ATT_V_DIM
    blk = lambda off: pl.BlockSpec((seq, LANES), lambda bi, h: (bi, off + h))
    return pl.pallas_call(
        functools.partial(_prompt_attn_kernel, tq=tq, lam_init=lam_init),
        grid=(nb, heads),
        in_specs=[pl.BlockSpec(memory_space=pltpu.SMEM), blk(0),
                  pl.BlockSpec((1, LANES, seq), lambda bi, h: (bi, h, 0)), blk(v_blk), blk(gate_blk),
                  pl.BlockSpec((1, LANES), lambda bi, h: (0, 0))],
        out_specs=blk(0),
        out_shape=jax.ShapeDtypeStruct((m, width), BF16),
        scratch_shapes=[pltpu.VMEM((LANES, seq), BF16), pltpu.VMEM((seq, 2 * LANES), BF16)],
        compiler_params=_cparams("parallel", "parallel"),
        name="prompt_attn",
    )(lam, qn, kt, proj, proj, sub_g.reshape(1, LANES))


def _sample_attn_kernel(pt_ref, lam_ref, q_ref, *rest, heads, dec, lam_init, group):
    kt_refs = rest[:group]
    v_refs = rest[group:2 * group]
    kn_ref, vn_ref, gate_ref, sg_ref, o_ref, m_ref, l_ref, acc_ref = rest[2 * group:]
    p = pl.program_id(1)
    rph = 2 * dec
    rows = heads * rph

    @pl.when(p == 0)
    def _():
        m_ref[...] = jnp.full(m_ref.shape, NEG, F32)
        l_ref[...] = jnp.zeros(l_ref.shape, F32)
        acc_ref[...] = jnp.zeros(acc_ref.shape, F32)

    def q_head(h):
        return q_ref[h * rph:(h + 1) * rph, :]

    def update(s, v_head, mask):
        if mask is not None:
            s = jnp.where(mask, s, NEG)
        m_new = jnp.maximum(m_ref[...], jnp.max(s, axis=-1, keepdims=True))
        alpha = jnp.exp2(m_ref[...] - m_new)
        pr = jnp.exp2(s - m_new)
        l_ref[...] = alpha * l_ref[...] + jnp.sum(pr, axis=-1, keepdims=True)
        prb = pr.astype(BF16)
        pv = [_dot(prb[h * rph:(h + 1) * rph, :], v_head(h)) for h in range(heads)]
        acc_ref[...] = alpha * acc_ref[...] + jnp.concatenate(pv, axis=0)
        m_ref[...] = m_new

    def kt_head(h):
        return jnp.concatenate([r[0, h * LANES:(h + 1) * LANES, :] for r in kt_refs], axis=1).astype(BF16)

    v_pages = [jnp.swapaxes(r[0].reshape(PAGE, heads, ATT_V_DIM), 0, 1) for r in v_refs]

    def v_head(h):
        return jnp.concatenate([vp[h] for vp in v_pages], axis=0).astype(BF16)

    update(jnp.concatenate([_dot(q_head(h), kt_head(h)) for h in range(heads)], axis=0), v_head, None)

    @pl.when(p == pl.num_programs(1) - 1)
    def _():
        zpad = jnp.zeros((PAGE - dec, LANES), F32)

        def new_head(ref, h):
            return jnp.concatenate([ref[:, h * LANES:(h + 1) * LANES], zpad], axis=0).astype(BF16)

        s = jnp.concatenate([_dot_nt(q_head(h), new_head(kn_ref, h)) for h in range(heads)], axis=0)
        row = lax.broadcasted_iota(jnp.int32, (rows, PAGE), 0)
        col = lax.broadcasted_iota(jnp.int32, (rows, PAGE), 1)
        update(s, functools.partial(new_head, vn_ref), col <= row % dec)
        o = acc_ref[...] / l_ref[...]
        lam = lam_ref[0]
        outs = []
        for h in range(heads):
            o1 = o[h * rph:h * rph + dec]
            o2 = o[h * rph + dec:(h + 1) * rph]
            outs.append(_diff_out(o1, o2, lam, lam_init, sg_ref[...], gate_ref[:, h * ATT_V_DIM:(h + 1) * ATT_V_DIM]))
        o_ref[...] = jnp.concatenate(outs, axis=1)


def _sample_attn(page_table, lam, qs, cache_kt, cache_v, kn, proj, sub_g, nb, dec, heads, *, page0, v_blk, gate_blk,
                 lam_init):
    width = heads * ATT_V_DIM
    rows = heads * 2 * dec
    n_pages = page_table.shape[1]
    group = math.gcd(n_pages, PAGE_GROUP)

    def page(i):
        return lambda bi, p, pt: (page0 + pt[bi, group * p + i], 0, 0)

    grid_spec = pltpu.PrefetchScalarGridSpec(
        num_scalar_prefetch=1,
        grid=(nb, n_pages // group),
        in_specs=([pl.BlockSpec(memory_space=pltpu.SMEM),
                   pl.BlockSpec((rows, LANES), lambda bi, p, pt: (bi, 0))]
                  + [pl.BlockSpec((1, width, PAGE), page(i)) for i in range(group)]
                  + [pl.BlockSpec((1, PAGE * heads, LANES), page(i)) for i in range(group)]
                  + [pl.BlockSpec((dec, width), lambda bi, p, pt: (bi, 0)),
                     pl.BlockSpec((dec, width), lambda bi, p, pt: (bi, v_blk)),
                     pl.BlockSpec((dec, width), lambda bi, p, pt: (bi, gate_blk)),
                     pl.BlockSpec((1, LANES), lambda bi, p, pt: (0, 0))]),
        out_specs=pl.BlockSpec((dec, width), lambda bi, p, pt: (bi, 0)),
        scratch_shapes=[pltpu.VMEM((rows, 1), F32), pltpu.VMEM((rows, 1), F32), pltpu.VMEM((rows, ATT_V_DIM), F32)],
    )
    return pl.pallas_call(
        functools.partial(_sample_attn_kernel, heads=heads, dec=dec, lam_init=lam_init, group=group),
        grid_spec=grid_spec,
        out_shape=jax.ShapeDtypeStruct((nb * dec, width), F32),
        compiler_params=_cparams("parallel", "arbitrary"),
        name="sample_attn",
    )(page_table, lam, qs, *([cache_kt] * group), *([cache_v] * group), kn, proj, proj, sub_g.reshape(1, LANES))


def _even_layer(x, nb, seq, conv_prev, ssd_conv_prev, ssd_prev, p, *, tm, tl, q, act_dtype):
    ch = p["conv_w"].shape[1]
    proj, dtp = _norm_matmul(x, p["norm_g"], p["w_main"], p["w_dt"], tm=tm, tn=1024)
    if conv_prev is not None:
        conv_prev = jnp.pad(conv_prev, ((0, 0), (CONV_PAD - (CONV_WIDTH - 1), 0), (0, 0)))
        ssd_conv_prev = jnp.pad(ssd_conv_prev, ((0, 0), (SSD_PAD - (SSD_CONV - 1), 0), (0, 0)))
        ssd_prev = ssd_prev.reshape(nb, -1, SSD_STATE)
    ya, conv_new = _conv_a(proj, nb, seq, conv_prev, p["conv_w"], p["conv_b"], p["ln_g"], p["ln_b"],
                           tl=tl, out_dtype=act_dtype)
    yb, ssd_conv_new, ssd_new = _ssd(proj, dtp, nb, seq, ssd_conv_prev, ssd_prev, p["ssd_conv_w"], p["ssd_conv_b"],
                                     p["dt_bias"], p["a_log"], p["d_skip"], p["ssd_norm_g"],
                                     col0=3 * ch, q=q, out_dtype=act_dtype)
    y = _matmul_res([ya, yb], [p["w_out_a"], p["w_out_b"]], x, tm=tm, tn=512)
    return y, conv_new, ssd_conv_new, ssd_new


def kernel(x_prompt, x_sample, state_conv_a, state_ssd_conv, state_ssd, cache_k, cache_v, page_table, norm_even, w_in_even, conv_a_w, conv_a_b, conv_a_ln_g, conv_a_ln_b, ssd_conv_w, ssd_conv_b, ssd_dt_bias, ssd_a_log, ssd_d, ssd_norm_g, w_out_even, norm_odd, w_in_odd, q_norm_g, k_norm_g, lam_q1, lam_k1, lam_q2, lam_k2, subln_g, w_out_odd):
    bp, seq, d = x_prompt.shape
    bs, dec, _ = x_sample.shape
    depth = norm_even.shape[0] + norm_odd.shape[0]
    xp = x_prompt.reshape(bp * seq, d)
    xs = x_sample.reshape(bs * dec, d)
    ch = conv_a_w.shape[2]
    ssd_heads = ssd_dt_bias.shape[1]
    inner = ssd_norm_g.shape[1]
    att_width = w_out_odd.shape[1]
    heads = att_width // ATT_V_DIM
    main_cols = w_in_even.shape[2] - ssd_heads

    tm_p = min(1024, bp * seq)
    tq = min(256, seq)

    ca_p, ca_s, sc_p, sc_s, ss_p, ss_s = [], [], [], [], [], []
    k_p, k_s, v_p, v_s = [], [], [], []
    for layer in range(depth):
        j = layer // 2
        if layer % 2 == 0:
            w_in = w_in_even[j]
            p = dict(norm_g=norm_even[j],
                     w_main=w_in[:, :main_cols].astype(BF16),
                     w_dt=jnp.pad(w_in[:, main_cols:], ((0, 0), (0, LANES - ssd_heads))).astype(BF16),
                     conv_w=conv_a_w[j], conv_b=conv_a_b[j], ln_g=conv_a_ln_g[j], ln_b=conv_a_ln_b[j],
                     ssd_conv_w=ssd_conv_w[j], ssd_conv_b=ssd_conv_b[j], dt_bias=ssd_dt_bias[j], a_log=ssd_a_log[j],
                     d_skip=ssd_d[j], ssd_norm_g=ssd_norm_g[j],
                     w_out_a=w_out_even[j][:ch].astype(BF16), w_out_b=w_out_even[j][ch:].astype(BF16))
            xp, c1, c2, c3 = _even_layer(xp, bp, seq, None, None, None, p, tm=tm_p, tl=256, q=SSD_CHUNK, act_dtype=BF16)
            xs, d1, d2, d3 = _even_layer(xs, bs, dec, state_conv_a[j], state_ssd_conv[j], state_ssd[j], p,
                                         tm=bs * dec, tl=dec, q=dec, act_dtype=F32)
            ca_p.append(c1)
            sc_p.append(c2)
            ss_p.append(c3.reshape(bp, ssd_heads, SSD_HEAD_DIM, SSD_STATE))
            ca_s.append(d1)
            sc_s.append(d2)
            ss_s.append(d3.reshape(bs, ssd_heads, SSD_HEAD_DIM, SSD_STATE))
        else:
            lam_init = 0.8 - 0.6 * math.exp(-0.3 * layer)
            lam = (jnp.exp(jnp.sum(lam_q1[j].astype(F32) * lam_k1[j].astype(F32)))
                   - jnp.exp(jnp.sum(lam_q2[j].astype(F32) * lam_k2[j].astype(F32))) + lam_init).reshape(1)
            w_in = w_in_odd[j].astype(BF16)
            w_out = w_out_odd[j].astype(BF16)
            v_blk, gate_blk = 2 * att_width // LANES, 3 * att_width // LANES

            proj = _norm_matmul(xp, norm_odd[j], w_in, tm=tm_p, tn=1024)
            qn, kt = _qk_prep_prompt(proj, q_norm_g[j], k_norm_g[j], bp, seq, att_width, tm=min(512, seq))
            o = _prompt_attn(lam, qn, kt, proj, subln_g[j], bp, seq, heads,
                             v_blk=v_blk, gate_blk=gate_blk, lam_init=lam_init, tq=tq)
            k_p.append(jnp.transpose(kt.reshape(bp, heads, 2, ATT_QK_DIM, seq), (0, 4, 1, 2, 3)))
            v_p.append(proj[:, 2 * att_width:3 * att_width].reshape(bp, seq, heads, ATT_V_DIM))
            xp = _matmul_res([o], [w_out], xp, tm=tm_p, tn=512)

            proj = _norm_matmul(xs, norm_odd[j], w_in, tm=bs * dec, tn=1024)
            qs, kn = _qk_prep_sample(proj, q_norm_g[j], k_norm_g[j], bs, dec, att_width)
            n_pool = cache_k.shape[1]
            cache_kt = jnp.transpose(cache_k, (0, 1, 3, 4, 5, 2)).reshape(-1, att_width, PAGE)
            o = _sample_attn(page_table, lam, qs, cache_kt, cache_v.reshape(-1, PAGE * heads, ATT_V_DIM), kn, proj,
                             subln_g[j], bs, dec, heads, page0=j * n_pool, v_blk=2, gate_blk=3, lam_init=lam_init)
            k_s.append(kn.reshape(bs, dec, heads, 2, ATT_QK_DIM))
            v_s.append(proj[:, 2 * att_width:3 * att_width].reshape(bs, dec, heads, ATT_V_DIM))
            xs = _matmul_res([o], [w_out], xs, tm=bs * dec, tn=512)
    return (xp.reshape(bp, seq, d), xs.reshape(bs, dec, d), jnp.stack(ca_p), jnp.stack(ca_s), jnp.stack(sc_p),
            jnp.stack(sc_s), jnp.stack(ss_p), jnp.stack(ss_s), jnp.stack(k_p), jnp.stack(k_s), jnp.stack(v_p),
            jnp.stack(v_s))
```

```python
import functools
import math

import jax
import jax.numpy as jnp
from jax import lax
from jax.experimental import pallas as pl
from jax.experimental.pallas import tpu as pltpu

F32 = jnp.float32
BF16 = jnp.bfloat16

EPS = 1e-6
NEG = -1e30
LANES = 128
SUBLANES = 8
CONV_WIDTH = 31
CONV_PAD = 32
SSD_HEAD_DIM = 64
SSD_GROUPS = 4
SSD_STATE = 128
SSD_CONV = 4
SSD_PAD = 8
SSD_CHUNK = 128
ATT_QK_DIM = 64
ATT_V_DIM = 128
PAGE = 128
PAGE_GROUP = 8
VMEM_LIMIT = 56 * 1024 * 1024


def _cparams(*sem):
    return pltpu.CompilerParams(dimension_semantics=sem, vmem_limit_bytes=VMEM_LIMIT)


def _sigmoid(x):
    return 0.5 * jnp.tanh(0.5 * x) + 0.5


def _silu(x):
    return x * _sigmoid(x)


def _split2(x):
    hi = x.astype(BF16)
    lo = (x - hi.astype(F32)).astype(BF16)
    return hi, lo


def _split3(x):
    hi = x.astype(BF16)
    r = x - hi.astype(F32)
    mid = r.astype(BF16)
    lo = (r - mid.astype(F32)).astype(BF16)
    return hi, mid, lo


def _dot(a, b):
    return jnp.dot(a, b, preferred_element_type=F32)


def _dot_nt(a, b):
    return lax.dot_general(a, b, (((1,), (1,)), ((), ())), preferred_element_type=F32)


def _dot_tn(a, b):
    return lax.dot_general(a, b, (((0,), (0,)), ((), ())), preferred_element_type=F32)


def _norm_matmul_kernel(x_ref, g_ref, w_ref, *rest, has_aux):
    if has_aux:
        wa_ref, o_ref, oa_ref, hn_ref = rest
    else:
        o_ref, hn_ref = rest

    @pl.when(pl.program_id(1) == 0)
    def _():
        x = x_ref[...]
        ms = jnp.mean(x * x, axis=-1, keepdims=True)
        hn_ref[...] = (x * lax.rsqrt(ms + EPS) * g_ref[...]).astype(BF16)
        if has_aux:
            oa_ref[...] = _dot(hn_ref[...], wa_ref[...])

    o_ref[...] = _dot(hn_ref[...], w_ref[...])


def _norm_matmul(x, g, w, w_aux=None, *, tm, tn):
    m, k = x.shape
    n = w.shape[1]
    assert m % tm == 0 and n % tn == 0
    has_aux = w_aux is not None
    in_specs = [pl.BlockSpec((tm, k), lambda i, j: (i, 0)),
                pl.BlockSpec((1, k), lambda i, j: (0, 0)),
                pl.BlockSpec((k, tn), lambda i, j: (0, j))]
    out_specs = [pl.BlockSpec((tm, tn), lambda i, j: (i, j))]
    out_shape = [jax.ShapeDtypeStruct((m, n), F32)]
    args = [x, g.reshape(1, k), w]
    if has_aux:
        na = w_aux.shape[1]
        in_specs.append(pl.BlockSpec((k, na), lambda i, j: (0, 0)))
        out_specs.append(pl.BlockSpec((tm, na), lambda i, j: (i, 0)))
        out_shape.append(jax.ShapeDtypeStruct((m, na), F32))
        args.append(w_aux)
    outs = pl.pallas_call(
        functools.partial(_norm_matmul_kernel, has_aux=has_aux),
        grid=(m // tm, n // tn),
        in_specs=in_specs, out_specs=out_specs, out_shape=out_shape,
        scratch_shapes=[pltpu.VMEM((tm, k), BF16)],
        compiler_params=_cparams("parallel", "arbitrary"),
        name="norm_matmul",
    )(*args)
    return outs if has_aux else outs[0]


def _matmul_res_kernel(*refs, n_parts):
    a_refs = refs[:n_parts]
    w_refs = refs[n_parts:2 * n_parts]
    res_ref, o_ref = refs[2 * n_parts], refs[2 * n_parts + 1]
    acc = res_ref[...]
    for a_ref, w_ref in zip(a_refs, w_refs):
        acc = acc + _dot(a_ref[...].astype(BF16), w_ref[...])
    o_ref[...] = acc


def _matmul_res(a_parts, w_parts, res, *, tm, tn):
    m, n = res.shape
    assert m % tm == 0 and n % tn == 0
    n_parts = len(a_parts)
    in_specs = ([pl.BlockSpec((tm, a.shape[1]), lambda i, j: (i, 0)) for a in a_parts]
                + [pl.BlockSpec((w.shape[0], tn), lambda i, j: (0, j)) for w in w_parts]
                + [pl.BlockSpec((tm, tn), lambda i, j: (i, j))])
    return pl.pallas_call(
        functools.partial(_matmul_res_kernel, n_parts=n_parts),
        grid=(m // tm, n // tn),
        in_specs=in_specs,
        out_specs=pl.BlockSpec((tm, tn), lambda i, j: (i, j)),
        out_shape=jax.ShapeDtypeStruct((m, n), F32),
        compiler_params=_cparams("parallel", "arbitrary"),
        name="matmul_res",
    )(*a_parts, *w_parts, res)


def _conv_a_kernel(val_ref, glu_ref, gate_ref, *rest, tl, has_prefix, row_blk, ln_blk, ln_par):
    if has_prefix:
        pre_ref, w_ref, b_ref, lg_ref, lb_ref, ya_ref, st_ref, ubuf, cbuf = rest
    else:
        w_ref, b_ref, lg_ref, lb_ref, ya_ref, st_ref, ubuf, cbuf = rest
    ch = val_ref.shape[1]
    nlb = ch // LANES
    off = CONV_PAD - (CONV_WIDTH - 1)
    lanes = [slice(c * LANES, (c + 1) * LANES) for c in range(nlb)]

    @pl.when(pl.program_id(1) == 0)
    def _():
        for c in range(nlb):
            ubuf[c, 0:CONV_PAD, :] = pre_ref[0, :, lanes[c]] if has_prefix else jnp.zeros((CONV_PAD, LANES), F32)

    for c in range(nlb):
        ubuf[c, CONV_PAD:CONV_PAD + tl, :] = val_ref[:, lanes[c]] * _sigmoid(glu_ref[:, lanes[c]])

    def lane_body(c, carry):
        c0 = pl.multiple_of(c * LANES, LANES)
        taps = [w_ref[k:k + 1, pl.ds(c0, LANES)] for k in range(CONV_WIDTH)]
        bias = b_ref[:, pl.ds(c0, LANES)]
        for r0 in range(0, tl, row_blk):
            acc = jnp.zeros((row_blk, LANES), F32) + bias
            for sh in range(SUBLANES):
                ks = list(range(sh, CONV_WIDTH, SUBLANES))
                span = row_blk + ks[-1] - sh
                win = ubuf[c, r0 + off + sh:r0 + off + sh + span, :]
                for k in ks:
                    acc = acc + win[k - sh:k - sh + row_blk] * taps[k]
            cbuf[c, r0:r0 + row_blk, :] = acc
        return carry

    lax.fori_loop(0, nlb, lane_body, 0)

    for c in range(nlb):
        tail = ubuf[c, tl:tl + CONV_PAD, :]
        st_ref[0, :, lanes[c]] = tail
        ubuf[c, 0:CONV_PAD, :] = tail

    def ln_body(r, carry):
        for j in range(ln_par):
            r0 = pl.multiple_of((r * ln_par + j) * ln_blk, ln_blk)
            u = jnp.concatenate([cbuf[c, pl.ds(r0, ln_blk), :] for c in range(nlb)], axis=1)
            mu = jnp.mean(u, axis=-1, keepdims=True)
            uc = u - mu
            var = jnp.mean(uc * uc, axis=-1, keepdims=True)
            y = uc * lax.rsqrt(var + EPS) * lg_ref[...] + lb_ref[...]
            ya = _silu(y) * _silu(gate_ref[pl.ds(r0, ln_blk), :])
            ya_ref[pl.ds(r0, ln_blk), :] = ya.astype(ya_ref.dtype)
        return carry

    lax.fori_loop(0, tl // (ln_blk * ln_par), ln_body, 0)


def _conv_a(proj, nb, seq, prefix, w, b, lg, lb, *, tl, out_dtype):
    ch = w.shape[1]
    nl = seq // tl
    has_prefix = prefix is not None
    row_blk = min(tl, 64)
    ln_blk = min(tl, 16)
    ln_par = max(1, min(4, tl // ln_blk))
    in_specs = [pl.BlockSpec((tl, ch), lambda bi, i: (bi * nl + i, 0)),
                pl.BlockSpec((tl, ch), lambda bi, i: (bi * nl + i, 1)),
                pl.BlockSpec((tl, ch), lambda bi, i: (bi * nl + i, 2))]
    args = [proj, proj, proj]
    if has_prefix:
        in_specs.append(pl.BlockSpec((1, CONV_PAD, ch), lambda bi, i: (bi, 0, 0)))
        args.append(prefix)
    in_specs += [pl.BlockSpec((CONV_WIDTH, ch), lambda bi, i: (0, 0))] + [pl.BlockSpec((1, ch), lambda bi, i: (0, 0))] * 3
    args += [w, b.reshape(1, ch), lg.reshape(1, ch), lb.reshape(1, ch)]
    ya, st = pl.pallas_call(
        functools.partial(_conv_a_kernel, tl=tl, has_prefix=has_prefix, row_blk=row_blk, ln_blk=ln_blk, ln_par=ln_par),
        grid=(nb, nl),
        in_specs=in_specs,
        out_specs=[pl.BlockSpec((tl, ch), lambda bi, i: (bi * nl + i, 0)),
                   pl.BlockSpec((1, CONV_PAD, ch), lambda bi, i: (bi, 0, 0))],
        out_shape=[jax.ShapeDtypeStruct((nb * seq, ch), out_dtype),
                   jax.ShapeDtypeStruct((nb, CONV_PAD, ch), F32)],
        scratch_shapes=[pltpu.VMEM((ch // LANES, CONV_PAD + tl, LANES), F32), pltpu.VMEM((ch // LANES, tl, LANES), F32)],
        compiler_params=_cparams("parallel", "arbitrary"),
        name="conv_a",
    )(*args)
    return ya, st[:, CONV_PAD - (CONV_WIDTH - 1):, :]


def _ssd_kernel(z_ref, xs_ref, bc_ref, dt_ref, *rest, q, qp, has_state):
    if has_state:
        pre_ref, h0_ref = rest[:2]
        rest = rest[2:]
    (cw_ref, cb_ref, dtb_ref, alog_ref, dsk_ref, ng_ref, e_ref, et_ref,
     yb_ref, cst_ref, hst_ref, xbuf, h_ref) = rest
    inner = xs_ref.shape[1]
    nbc = bc_ref.shape[1]
    xbc_w = inner + nbc
    gw = inner // SSD_GROUPS
    hpg = gw // SSD_HEAD_DIM
    off = SSD_PAD - (SSD_CONV - 1)

    nlb = xbc_w // LANES
    lanes = [slice(c * LANES, (c + 1) * LANES) for c in range(nlb)]

    @pl.when(pl.program_id(1) == 0)
    def _():
        for c in range(nlb):
            xbuf[c, 0:SSD_PAD, :] = pre_ref[0, :, lanes[c]] if has_state else jnp.zeros((SSD_PAD, LANES), F32)
        h_ref[...] = h0_ref[0] if has_state else jnp.zeros(h_ref.shape, F32)

    conv_parts = []
    for c in range(nlb):
        src = xs_ref[:, lanes[c]] if c < inner // LANES else bc_ref[:, c * LANES - inner:(c + 1) * LANES - inner]
        xbuf[c, SSD_PAD:SSD_PAD + q, :] = src
        acc = jnp.zeros((q, LANES), F32) + cb_ref[:, lanes[c]]
        for k in range(SSD_CONV):
            acc = acc + xbuf[c, off + k:off + k + q, :] * cw_ref[k:k + 1, lanes[c]]
        conv_parts.append(_silu(acc))
        tail = xbuf[c, q:q + SSD_PAD, :]
        cst_ref[0, :, lanes[c]] = tail
        xbuf[c, 0:SSD_PAD, :] = tail
    xbc = jnp.concatenate(conv_parts, axis=1)

    dt_raw = dt_ref[...] + dtb_ref[...]
    dt = jnp.maximum(dt_raw, 0.0) + jnp.log(1.0 + jnp.exp(-jnp.abs(dt_raw)))
    zg = z_ref[...]
    if q < qp:
        xbc = jnp.concatenate([xbc, jnp.zeros((qp - q, xbc_w), F32)], axis=0)
        dt = jnp.concatenate([dt, jnp.zeros((qp - q, LANES), F32)], axis=0)
        zg = jnp.concatenate([zg, jnp.zeros((qp - q, inner), F32)], axis=0)
    x = xbc[:, 0:inner]

    a_neg = -jnp.exp(alog_ref[...])
    dta = dt * a_neg
    row = lax.broadcasted_iota(jnp.int32, (qp, qp), 0)
    col = lax.broadcasted_iota(jnp.int32, (qp, qp), 1)
    causal = row >= col
    tril = jnp.where(causal, 1.0, 0.0).astype(BF16)
    acs = sum(_dot(tril, part) for part in _split3(dta))
    acs_t = acs.T
    last = acs[qp - 1:qp, :]
    eat = jnp.exp(acs)
    ddt = jnp.exp(last - acs) * dt

    stack = jnp.concatenate([dt, ddt, eat], axis=0)
    ex = sum(_dot(part, e_ref[...]) for part in _split2(stack))
    xdt = (x * ex[0:qp]).astype(BF16)
    xdd = (x * ex[qp:2 * qp]).astype(BF16)
    eat_x = ex[2 * qp:3 * qp]

    lane = lax.broadcasted_iota(jnp.int32, (qp, LANES), 1)
    lo_half = lane < SSD_HEAD_DIM
    y_parts, s_parts = [], []
    for g in range(SSD_GROUPS):
        bg = xbc[:, inner + g * SSD_STATE:inner + (g + 1) * SSD_STATE].astype(BF16)
        cg = xbc[:, inner + (SSD_GROUPS + g) * SSD_STATE:inner + (SSD_GROUPS + g + 1) * SSD_STATE].astype(BF16)
        cb = _dot_nt(cg, bg)
        hg = h_ref[g * gw:(g + 1) * gw, :].astype(BF16)
        y_off = _dot_nt(cg, hg) * eat_x[:, g * gw:(g + 1) * gw]
        s_parts.append(_dot_tn(xdd[:, g * gw:(g + 1) * gw], bg))
        yd = []
        for j in range(hpg // 2):
            h0 = g * hpg + 2 * j
            ms = []
            for h in (h0, h0 + 1):
                seg = jnp.where(causal, acs[:, h:h + 1] - acs_t[h:h + 1, :], -jnp.inf)
                ms.append(cb * jnp.exp(seg))
            lhs = jnp.concatenate(ms, axis=1).astype(BF16)
            xp = xdt[:, h0 * SSD_HEAD_DIM:h0 * SSD_HEAD_DIM + LANES]
            zero = jnp.zeros_like(xp)
            rhs = jnp.concatenate([jnp.where(lo_half, xp, zero), jnp.where(lo_half, zero, xp)], axis=0)
            yd.append(_dot(lhs, rhs))
        y_parts.append(jnp.concatenate(yd, axis=1) + y_off)
    y = jnp.concatenate(y_parts, axis=1) + dsk_ref[...] * x

    dec = jnp.broadcast_to(jnp.exp(acs_t[:, qp - 1:qp]), (LANES, LANES))
    dec_rows = sum(_dot(et_ref[...], part) for part in _split2(dec))
    h_new = h_ref[...] * dec_rows + jnp.concatenate(s_parts, axis=0)
    h_ref[...] = h_new
    hst_ref[0] = h_new

    y = y * _silu(zg)
    outs = []
    for g in range(SSD_GROUPS):
        yg = y[:, g * gw:(g + 1) * gw]
        outs.append(yg * lax.rsqrt(jnp.mean(yg * yg, axis=-1, keepdims=True) + EPS))
    yn = jnp.concatenate(outs, axis=1) * ng_ref[...]
    yb_ref[...] = yn[0:q].astype(yb_ref.dtype)


def _ssd(proj, dtp, nb, seq, prefix, h0, cw, cb, dt_bias, a_log, d_skip, norm_g, *, col0, q, out_dtype):
    inner = norm_g.shape[0]
    heads = dt_bias.shape[0]
    nbc = 2 * SSD_GROUPS * SSD_STATE
    xbc_w = inner + nbc
    nc = seq // q
    qp = SSD_CHUNK
    has_state = h0 is not None
    zb, xb, bcb = col0 // inner, col0 // inner + 1, (col0 + 2 * inner) // nbc
    assert col0 % inner == 0 and (col0 + 2 * inner) % nbc == 0

    lane_head = jnp.arange(inner, dtype=jnp.int32) // SSD_HEAD_DIM
    expand = (jnp.arange(LANES, dtype=jnp.int32)[:, None] == lane_head[None, :]).astype(BF16)
    pad = LANES - heads
    dtb = jnp.pad(dt_bias.astype(F32), (0, pad)).reshape(1, LANES)
    alog = jnp.pad(a_log.astype(F32), (0, pad)).reshape(1, LANES)
    dsk = jnp.repeat(d_skip.astype(F32), SSD_HEAD_DIM).reshape(1, inner)

    in_specs = [pl.BlockSpec((q, inner), lambda bi, c: (bi * nc + c, zb)),
                pl.BlockSpec((q, inner), lambda bi, c: (bi * nc + c, xb)),
                pl.BlockSpec((q, nbc), lambda bi, c: (bi * nc + c, bcb)),
                pl.BlockSpec((q, LANES), lambda bi, c: (bi * nc + c, 0))]
    args = [proj, proj, proj, dtp]
    if has_state:
        in_specs += [pl.BlockSpec((1, SSD_PAD, xbc_w), lambda bi, c: (bi, 0, 0)),
                     pl.BlockSpec((1, inner, SSD_STATE), lambda bi, c: (bi, 0, 0))]
        args += [prefix, h0]
    const = lambda bi, c: (0, 0)
    in_specs += [pl.BlockSpec((SSD_CONV, xbc_w), const), pl.BlockSpec((1, xbc_w), const),
                 pl.BlockSpec((1, LANES), const), pl.BlockSpec((1, LANES), const),
                 pl.BlockSpec((1, inner), const), pl.BlockSpec((1, inner), const),
                 pl.BlockSpec((LANES, inner), const), pl.BlockSpec((inner, LANES), const)]
    args += [cw, cb.reshape(1, xbc_w), dtb, alog, dsk, norm_g.reshape(1, inner), expand, expand.T]
    yb, cst, hst = pl.pallas_call(
        functools.partial(_ssd_kernel, q=q, qp=qp, has_state=has_state),
        grid=(nb, nc),
        in_specs=in_specs,
        out_specs=[pl.BlockSpec((q, inner), lambda bi, c: (bi * nc + c, 0)),
                   pl.BlockSpec((1, SSD_PAD, xbc_w), lambda bi, c: (bi, 0, 0)),
                   pl.BlockSpec((1, inner, SSD_STATE), lambda bi, c: (bi, 0, 0))],
        out_shape=[jax.ShapeDtypeStruct((nb * seq, inner), out_dtype),
                   jax.ShapeDtypeStruct((nb, SSD_PAD, xbc_w), F32),
                   jax.ShapeDtypeStruct((nb, inner, SSD_STATE), F32)],
        scratch_shapes=[pltpu.VMEM((xbc_w // LANES, SSD_PAD + q, LANES), F32), pltpu.VMEM((inner, SSD_STATE), F32)],
        compiler_params=_cparams("parallel", "arbitrary"),
        name="ssd",
    )(*args)
    return yb, cst[:, SSD_PAD - (SSD_CONV - 1):, :], hst


def _seg_rms(xb, gain, bd):
    ss = sum(_dot(part, bd) for part in _split2(xb * xb))
    return xb * lax.rsqrt(ss * (1.0 / ATT_QK_DIM) + EPS) * gain


Q_SCALE = ATT_QK_DIM ** -0.5 * math.log2(math.e)


def _qk_prep_prompt_kernel(q_ref, k_ref, qg_ref, kg_ref, bd_ref, qo_ref, kt_ref):
    width = q_ref.shape[1]
    bd = bd_ref[...]
    for c in range(width // LANES):
        cs = slice(c * LANES, (c + 1) * LANES)
        kt_ref[0, cs, :] = _seg_rms(k_ref[:, cs], kg_ref[...], bd).T
        qo_ref[:, cs] = (_seg_rms(q_ref[:, cs], qg_ref[...], bd) * Q_SCALE).astype(qo_ref.dtype)


def _qk_prep_sample_kernel(q_ref, k_ref, qg_ref, kg_ref, bd_ref, qo_ref, ko_ref):
    rows, width = q_ref.shape
    bd = bd_ref[...]
    lo_half = lax.broadcasted_iota(jnp.int32, (rows, LANES), 1) < ATT_QK_DIM
    blocks = []
    for c in range(width // LANES):
        cs = slice(c * LANES, (c + 1) * LANES)
        ko_ref[:, cs] = _seg_rms(k_ref[:, cs], kg_ref[...], bd)
        qn = _seg_rms(q_ref[:, cs], qg_ref[...], bd) * Q_SCALE
        blocks += [jnp.where(lo_half, qn, 0.0), jnp.where(lo_half, 0.0, qn)]
    qo_ref[...] = jnp.concatenate(blocks, axis=0).astype(qo_ref.dtype)


def _qk_consts(q_g, k_g):
    seg = jnp.arange(LANES, dtype=jnp.int32) // ATT_QK_DIM
    bd = (seg[:, None] == seg[None, :]).astype(BF16)
    qg = jnp.tile(q_g.astype(F32), LANES // ATT_QK_DIM).reshape(1, LANES)
    kg = jnp.tile(k_g.astype(F32), LANES // ATT_QK_DIM).reshape(1, LANES)
    return qg, kg, bd


def _qk_prep_prompt(proj, q_g, k_g, nb, seq, width, *, tm):
    nl = seq // tm
    const = lambda i: (0, 0)
    return pl.pallas_call(
        _qk_prep_prompt_kernel,
        grid=(nb * nl,),
        in_specs=[pl.BlockSpec((tm, width), lambda i: (i, 0)), pl.BlockSpec((tm, width), lambda i: (i, 1)),
                  pl.BlockSpec((1, LANES), const), pl.BlockSpec((1, LANES), const), pl.BlockSpec((LANES, LANES), const)],
        out_specs=[pl.BlockSpec((tm, width), lambda i: (i, 0)),
                   pl.BlockSpec((1, width, tm), lambda i: (i // nl, 0, i % nl))],
        out_shape=[jax.ShapeDtypeStruct((nb * seq, width), BF16), jax.ShapeDtypeStruct((nb, width, seq), F32)],
        compiler_params=_cparams("parallel"),
        name="qk_prep_prompt",
    )(proj, proj, *_qk_consts(q_g, k_g))


def _qk_prep_sample(proj, q_g, k_g, nb, dec, width):
    q_rows = 2 * (width // LANES) * dec
    const = lambda i: (0, 0)
    return pl.pallas_call(
        _qk_prep_sample_kernel,
        grid=(nb,),
        in_specs=[pl.BlockSpec((dec, width), lambda i: (i, 0)), pl.BlockSpec((dec, width), lambda i: (i, 1)),
                  pl.BlockSpec((1, LANES), const), pl.BlockSpec((1, LANES), const), pl.BlockSpec((LANES, LANES), const)],
        out_specs=[pl.BlockSpec((q_rows, LANES), lambda i: (i, 0)), pl.BlockSpec((dec, width), lambda i: (i, 0))],
        out_shape=[jax.ShapeDtypeStruct((nb * q_rows, LANES), BF16), jax.ShapeDtypeStruct((nb * dec, width), F32)],
        compiler_params=_cparams("parallel"),
        name="qk_prep_sample",
    )(proj, proj, *_qk_consts(q_g, k_g))


def _diff_out(o1, o2, lam, lam_init, sub_g, gate):
    d = o1 - lam * o2
    d = d * lax.rsqrt(jnp.mean(d * d, axis=-1, keepdims=True) + EPS) * sub_g * (1.0 - lam_init)
    return d * _silu(gate)


def _prompt_attn_kernel(lam_ref, q_ref, kt_ref, v_ref, gate_ref, sg_ref, o_ref, ktb_ref, vb_ref, *, tq, lam_init):
    seq = q_ref.shape[0]
    lam = lam_ref[0]
    ktb_ref[...] = kt_ref[0].astype(BF16)
    vb_ref[:, 0:LANES] = v_ref[...].astype(BF16)
    vb_ref[:, LANES:2 * LANES] = jnp.ones((seq, LANES), BF16)
    lane = lax.broadcasted_iota(jnp.int32, (tq, LANES), 1)
    lo_half = lane < ATT_QK_DIM
    row = lax.broadcasted_iota(jnp.int32, (2 * tq, tq), 0)
    col = lax.broadcasted_iota(jnp.int32, (2 * tq, tq), 1)
    diag_ok = col <= jnp.where(row >= tq, row - tq, row)
    def scores(qi):
        lo = qi * tq
        q = q_ref[lo:lo + tq, :]
        zero = jnp.zeros_like(q)
        q2 = jnp.concatenate([jnp.where(lo_half, q, zero), jnp.where(lo_half, zero, q)], axis=0)
        sd = jnp.where(diag_ok, _dot(q2, ktb_ref[:, lo:lo + tq]), NEG)
        sp = _dot(q2, ktb_ref[:, 0:lo]) if qi > 0 else None
        return sd, sp

    def finish(qi, sd, sp):
        lo = qi * tq
        m = jnp.max(sd, axis=-1, keepdims=True)
        if qi > 0:
            m = jnp.maximum(m, jnp.max(sp, axis=-1, keepdims=True))
        acc = _dot(jnp.exp2(sd - m).astype(BF16), vb_ref[lo:lo + tq, :])
        if qi > 0:
            acc = acc + _dot(jnp.exp2(sp - m).astype(BF16), vb_ref[0:lo, :])
        o = acc[:, 0:LANES] / acc[:, LANES:2 * LANES]
        out = _diff_out(o[0:tq], o[tq:2 * tq], lam, lam_init, sg_ref[...], gate_ref[lo:lo + tq, :])
        o_ref[lo:lo + tq, :] = out.astype(o_ref.dtype)

    nq = seq // tq
    cur = scores(0)
    for qi in range(nq):
        nxt = scores(qi + 1) if qi + 1 < nq else None
        finish(qi, *cur)
        cur = nxt


def _prompt_attn(lam, qn, kt, proj, sub_g, nb, seq, heads, *, v_blk, gate_blk, lam_init, tq):
    m = nb * seq
    width = heads * ATT_V_DIM
    blk = lambda off: pl.BlockSpec((seq, LANES), lambda bi, h: (bi, off + h))
    return pl.pallas_call(
        functools.partial(_prompt_attn_kernel, tq=tq, lam_init=lam_init),
        grid=(nb, heads),
        in_specs=[pl.BlockSpec(memory_space=pltpu.SMEM), blk(0),
                  pl.BlockSpec((1, LANES, seq), lambda bi, h: (bi, h, 0)), blk(v_blk), blk(gate_blk),
                  pl.BlockSpec((1, LANES), lambda bi, h: (0, 0))],
        out_specs=blk(0),
        out_shape=jax.ShapeDtypeStruct((m, width), BF16),
        scratch_shapes=[pltpu.VMEM((LANES, seq), BF16), pltpu.VMEM((seq, 2 * LANES), BF16)],
        compiler_params=_cparams("parallel", "parallel"),
        name="prompt_attn",
    )(lam, qn, kt, proj, proj, sub_g.reshape(1, LANES))


def _sample_attn_kernel(pt_ref, lam_ref, q_ref, *rest, heads, dec, lam_init, group):
    kt_refs = rest[:group]
    v_refs = rest[group:2 * group]
    kn_ref, vn_ref, gate_ref, sg_ref, o_ref, m_ref, l_ref, acc_ref = rest[2 * group:]
    p = pl.program_id(1)
    rph = 2 * dec
    rows = heads * rph

    @pl.when(p == 0)
    def _():
        m_ref[...] = jnp.full(m_ref.shape, NEG, F32)
        l_ref[...] = jnp.zeros(l_ref.shape, F32)
        acc_ref[...] = jnp.zeros(acc_ref.shape, F32)

    def q_head(h):
        return q_ref[h * rph:(h + 1) * rph, :]

    def update(s, v_head, mask):
        if mask is not None:
            s = jnp.where(mask, s, NEG)
        m_new = jnp.maximum(m_ref[...], jnp.max(s, axis=-1, keepdims=True))
        alpha = jnp.exp2(m_ref[...] - m_new)
        pr = jnp.exp2(s - m_new)
        l_ref[...] = alpha * l_ref[...] + jnp.sum(pr, axis=-1, keepdims=True)
        prb = pr.astype(BF16)
        pv = [_dot(prb[h * rph:(h + 1) * rph, :], v_head(h)) for h in range(heads)]
        acc_ref[...] = alpha * acc_ref[...] + jnp.concatenate(pv, axis=0)
        m_ref[...] = m_new

    def kt_head(h):
        return jnp.concatenate([r[0, h * LANES:(h + 1) * LANES, :] for r in kt_refs], axis=1).astype(BF16)

    v_pages = [jnp.swapaxes(r[0].reshape(PAGE, heads, ATT_V_DIM), 0, 1) for r in v_refs]

    def v_head(h):
        return jnp.concatenate([vp[h] for vp in v_pages], axis=0).astype(BF16)

    update(jnp.concatenate([_dot(q_head(h), kt_head(h)) for h in range(heads)], axis=0), v_head, None)

    @pl.when(p == pl.num_programs(1) - 1)
    def _():
        zpad = jnp.zeros((PAGE - dec, LANES), F32)

        def new_head(ref, h):
            return jnp.concatenate([ref[:, h * LANES:(h + 1) * LANES], zpad], axis=0).astype(BF16)

        s = jnp.concatenate([_dot_nt(q_head(h), new_head(kn_ref, h)) for h in range(heads)], axis=0)
        row = lax.broadcasted_iota(jnp.int32, (rows, PAGE), 0)
        col = lax.broadcasted_iota(jnp.int32, (rows, PAGE), 1)
        update(s, functools.partial(new_head, vn_ref), col <= row % dec)
        o = acc_ref[...] / l_ref[...]
        lam = lam_ref[0]
        outs = []
        for h in range(heads):
            o1 = o[h * rph:h * rph + dec]
            o2 = o[h * rph + dec:(h + 1) * rph]
            outs.append(_diff_out(o1, o2, lam, lam_init, sg_ref[...], gate_ref[:, h * ATT_V_DIM:(h + 1) * ATT_V_DIM]))
        o_ref[...] = jnp.concatenate(outs, axis=1)


def _sample_attn(page_table, lam, qs, cache_kt, cache_v, kn, proj, sub_g, nb, dec, heads, *, page0, v_blk, gate_blk,
                 lam_init):
    width = heads * ATT_V_DIM
    rows = heads * 2 * dec
    n_pages = page_table.shape[1]
    group = math.gcd(n_pages, PAGE_GROUP)

    def page(i):
        return lambda bi, p, pt: (page0 + pt[bi, group * p + i], 0, 0)

    grid_spec = pltpu.PrefetchScalarGridSpec(
        num_scalar_prefetch=1,
        grid=(nb, n_pages // group),
        in_specs=([pl.BlockSpec(memory_space=pltpu.SMEM),
                   pl.BlockSpec((rows, LANES), lambda bi, p, pt: (bi, 0))]
                  + [pl.BlockSpec((1, width, PAGE), page(i)) for i in range(group)]
                  + [pl.BlockSpec((1, PAGE * heads, LANES), page(i)) for i in range(group)]
                  + [pl.BlockSpec((dec, width), lambda bi, p, pt: (bi, 0)),
                     pl.BlockSpec((dec, width), lambda bi, p, pt: (bi, v_blk)),
                     pl.BlockSpec((dec, width), lambda bi, p, pt: (bi, gate_blk)),
                     pl.BlockSpec((1, LANES), lambda bi, p, pt: (0, 0))]),
        out_specs=pl.BlockSpec((dec, width), lambda bi, p, pt: (bi, 0)),
        scratch_shapes=[pltpu.VMEM((rows, 1), F32), pltpu.VMEM((rows, 1), F32), pltpu.VMEM((rows, ATT_V_DIM), F32)],
    )
    return pl.pallas_call(
        functools.partial(_sample_attn_kernel, heads=heads, dec=dec, lam_init=lam_init, group=group),
        grid_spec=grid_spec,
        out_shape=jax.ShapeDtypeStruct((nb * dec, width), F32),
        compiler_params=_cparams("parallel", "arbitrary"),
        name="sample_attn",
    )(page_table, lam, qs, *([cache_kt] * group), *([cache_v] * group), kn, proj, proj, sub_g.reshape(1, LANES))


def _even_layer(x, nb, seq, conv_prev, ssd_conv_prev, ssd_prev, p, *, tm, tl, q, act_dtype):
    ch = p["conv_w"].shape[1]
    proj, dtp = _norm_matmul(x, p["norm_g"], p["w_main"], p["w_dt"], tm=tm, tn=1024)
    if conv_prev is not None:
        conv_prev = jnp.pad(conv_prev, ((0, 0), (CONV_PAD - (CONV_WIDTH - 1), 0), (0, 0)))
        ssd_conv_prev = jnp.pad(ssd_conv_prev, ((0, 0), (SSD_PAD - (SSD_CONV - 1), 0), (0, 0)))
        ssd_prev = ssd_prev.reshape(nb, -1, SSD_STATE)
    ya, conv_new = _conv_a(proj, nb, seq, conv_prev, p["conv_w"], p["conv_b"], p["ln_g"], p["ln_b"],
                           tl=tl, out_dtype=act_dtype)
    yb, ssd_conv_new, ssd_new = _ssd(proj, dtp, nb, seq, ssd_conv_prev, ssd_prev, p["ssd_conv_w"], p["ssd_conv_b"],
                                     p["dt_bias"], p["a_log"], p["d_skip"], p["ssd_norm_g"],
                                     col0=3 * ch, q=q, out_dtype=act_dtype)
    y = _matmul_res([ya, yb], [p["w_out_a"], p["w_out_b"]], x, tm=tm, tn=512)
    return y, conv_new, ssd_conv_new, ssd_new


def kernel(x_prompt, x_sample, state_conv_a, state_ssd_conv, state_ssd, cache_k, cache_v, page_table, norm_even, w_in_even, conv_a_w, conv_a_b, conv_a_ln_g, conv_a_ln_b, ssd_conv_w, ssd_conv_b, ssd_dt_bias, ssd_a_log, ssd_d, ssd_norm_g, w_out_even, norm_odd, w_in_odd, q_norm_g, k_norm_g, lam_q1, lam_k1, lam_q2, lam_k2, subln_g, w_out_odd):
    bp, seq, d = x_prompt.shape
    bs, dec, _ = x_sample.shape
    depth = norm_even.shape[0] + norm_odd.shape[0]
    xp = x_prompt.reshape(bp * seq, d)
    xs = x_sample.reshape(bs * dec, d)
    ch = conv_a_w.shape[2]
    ssd_heads = ssd_dt_bias.shape[1]
    inner = ssd_norm_g.shape[1]
    att_width = w_out_odd.shape[1]
    heads = att_width // ATT_V_DIM
    main_cols = w_in_even.shape[2] - ssd_heads

    tm_p = min(1024, bp * seq)
    tq = min(256, seq)

    ca_p, ca_s, sc_p, sc_s, ss_p, ss_s = [], [], [], [], [], []
    k_p, k_s, v_p, v_s = [], [], [], []
    for layer in range(depth):
        j = layer // 2
        if layer % 2 == 0:
            w_in = w_in_even[j]
            p = dict(norm_g=norm_even[j],
                     w_main=w_in[:, :main_cols].astype(BF16),
                     w_dt=jnp.pad(w_in[:, main_cols:], ((0, 0), (0, LANES - ssd_heads))).astype(BF16),
                     conv_w=conv_a_w[j], conv_b=conv_a_b[j], ln_g=conv_a_ln_g[j], ln_b=conv_a_ln_b[j],
                     ssd_conv_w=ssd_conv_w[j], ssd_conv_b=ssd_conv_b[j], dt_bias=ssd_dt_bias[j], a_log=ssd_a_log[j],
                     d_skip=ssd_d[j], ssd_norm_g=ssd_norm_g[j],
                     w_out_a=w_out_even[j][:ch].astype(BF16), w_out_b=w_out_even[j][ch:].astype(BF16))
            xp, c1, c2, c3 = _even_layer(xp, bp, seq, None, None, None, p, tm=tm_p, tl=256, q=SSD_CHUNK, act_dtype=BF16)
            xs, d1, d2, d3 = _even_layer(xs, bs, dec, state_conv_a[j], state_ssd_conv[j], state_ssd[j], p,
                                         tm=bs * dec, tl=dec, q=dec, act_dtype=F32)
            ca_p.append(c1)
            sc_p.append(c2)
            ss_p.append(c3.reshape(bp, ssd_heads, SSD_HEAD_DIM, SSD_STATE))
            ca_s.append(d1)
            sc_s.append(d2)
            ss_s.append(d3.reshape(bs, ssd_heads, SSD_HEAD_DIM, SSD_STATE))
        else:
            lam_init = 0.8 - 0.6 * math.exp(-0.3 * layer)
            lam = (jnp.exp(jnp.sum(lam_q1[j].astype(F32) * lam_k1[j].astype(F32)))
                   - jnp.exp(jnp.sum(lam_q2[j].astype(F32) * lam_k2[j].astype(F32))) + lam_init).reshape(1)
            w_in = w_in_odd[j].astype(BF16)
            w_out = w_out_odd[j].astype(BF16)
            v_blk, gate_blk = 2 * att_width // LANES, 3 * att_width // LANES

            proj = _norm_matmul(xp, norm_odd[j], w_in, tm=tm_p, tn=1024)
            qn, kt = _qk_prep_prompt(proj, q_norm_g[j], k_norm_g[j], bp, seq, att_width, tm=min(512, seq))
            o = _prompt_attn(lam, qn, kt, proj, subln_g[j], bp, seq, heads,
                             v_blk=v_blk, gate_blk=gate_blk, lam_init=lam_init, tq=tq)
            k_p.append(jnp.transpose(kt.reshape(bp, heads, 2, ATT_QK_DIM, seq), (0, 4, 1, 2, 3)))
            v_p.append(proj[:, 2 * att_width:3 * att_width].reshape(bp, seq, heads, ATT_V_DIM))
            xp = _matmul_res([o], [w_out], xp, tm=tm_p, tn=512)

            proj = _norm_matmul(xs, norm_odd[j], w_in, tm=bs * dec, tn=1024)
            qs, kn = _qk_prep_sample(proj, q_norm_g[j], k_norm_g[j], bs, dec, att_width)
            n_pool = cache_k.shape[1]
            cache_kt = jnp.transpose(cache_k, (0, 1, 3, 4, 5, 2)).reshape(-1, att_width, PAGE)
            o = _sample_attn(page_table, lam, qs, cache_kt, cache_v.reshape(-1, PAGE * heads, ATT_V_DIM), kn, proj,
                             subln_g[j], bs, dec, heads, page0=j * n_pool, v_blk=2, gate_blk=3, lam_init=lam_init)
            k_s.append(kn.reshape(bs, dec, heads, 2, ATT_QK_DIM))
            v_s.append(proj[:, 2 * att_width:3 * att_width].reshape(bs, dec, heads, ATT_V_DIM))
            xs = _matmul_res([o], [w_out], xs, tm=bs * dec, tn=512)
    return (xp.reshape(bp, seq, d), xs.reshape(bs, dec, d), jnp.stack(ca_p), jnp.stack(ca_s), jnp.stack(sc_p),
            jnp.stack(sc_s), jnp.stack(ss_p), jnp.stack(ss_s), jnp.stack(k_p), jnp.stack(k_s), jnp.stack(v_p),
            jnp.stack(v_s))
```

```python
import functools
import math

import jax
import jax.numpy as jnp
from jax import lax
from jax.experimental import pallas as pl
from jax.experimental.pallas import tpu as pltpu

F32 = jnp.float32
BF16 = jnp.bfloat16

EPS = 1e-6
NEG = -1e30
LANES = 128
SUBLANES = 8
CONV_WIDTH = 31
CONV_PAD = 32
SSD_HEAD_DIM = 64
SSD_GROUPS = 4
SSD_STATE = 128
SSD_CONV = 4
SSD_PAD = 8
SSD_CHUNK = 128
ATT_QK_DIM = 64
ATT_V_DIM = 128
PAGE = 128
PAGE_GROUP = 8
VMEM_LIMIT = 56 * 1024 * 1024


def _cparams(*sem):
    return pltpu.CompilerParams(dimension_semantics=sem, vmem_limit_bytes=VMEM_LIMIT)


def _sigmoid(x):
    return 0.5 * jnp.tanh(0.5 * x) + 0.5


def _silu(x):
    return x * _sigmoid(x)


def _split2(x):
    hi = x.astype(BF16)
    lo = (x - hi.astype(F32)).astype(BF16)
    return hi, lo


def _split3(x):
    hi = x.astype(BF16)
    r = x - hi.astype(F32)
    mid = r.astype(BF16)
    lo = (r - mid.astype(F32)).astype(BF16)
    return hi, mid, lo


def _col_blocks(w, tn):
    k, n = w.shape
    return jnp.transpose(w.reshape(k, n // tn, tn), (1, 0, 2))


def _dot(a, b):
    return jnp.dot(a, b, preferred_element_type=F32)


def _dot_nt(a, b):
    return lax.dot_general(a, b, (((1,), (1,)), ((), ())), preferred_element_type=F32)


def _dot_tn(a, b):
    return lax.dot_general(a, b, (((0,), (0,)), ((), ())), preferred_element_type=F32)


def _norm_matmul_kernel(x_ref, g_ref, w_ref, *rest, has_aux):
    if has_aux:
        wa_ref, o_ref, oa_ref, hn_ref = rest
    else:
        o_ref, hn_ref = rest

    @pl.when(pl.program_id(1) == 0)
    def _():
        x = x_ref[...]
        ms = jnp.mean(x * x, axis=-1, keepdims=True)
        hn_ref[...] = (x * lax.rsqrt(ms + EPS) * g_ref[...]).astype(BF16)
        if has_aux:
            oa_ref[...] = _dot(hn_ref[...], wa_ref[...])

    o_ref[...] = _dot(hn_ref[...], w_ref[...])


def _norm_matmul(x, g, w, w_aux=None, *, tm, tn):
    m, k = x.shape
    n = w.shape[1]
    assert m % tm == 0 and n % tn == 0
    has_aux = w_aux is not None
    in_specs = [pl.BlockSpec((tm, k), lambda i, j: (i, 0)),
                pl.BlockSpec((1, k), lambda i, j: (0, 0)),
                pl.BlockSpec((None, k, tn), lambda i, j: (j, 0, 0))]
    out_specs = [pl.BlockSpec((tm, tn), lambda i, j: (i, j))]
    out_shape = [jax.ShapeDtypeStruct((m, n), F32)]
    args = [x, g.reshape(1, k), _col_blocks(w, tn)]
    if has_aux:
        na = w_aux.shape[1]
        in_specs.append(pl.BlockSpec((k, na), lambda i, j: (0, 0)))
        out_specs.append(pl.BlockSpec((tm, na), lambda i, j: (i, 0)))
        out_shape.append(jax.ShapeDtypeStruct((m, na), F32))
        args.append(w_aux)
    outs = pl.pallas_call(
        functools.partial(_norm_matmul_kernel, has_aux=has_aux),
        grid=(m // tm, n // tn),
        in_specs=in_specs, out_specs=out_specs, out_shape=out_shape,
        scratch_shapes=[pltpu.VMEM((tm, k), BF16)],
        compiler_params=_cparams("parallel", "arbitrary"),
        name="norm_matmul",
    )(*args)
    return outs if has_aux else outs[0]


def _matmul_res_kernel(*refs, n_parts):
    a_refs = refs[:n_parts]
    w_refs = refs[n_parts:2 * n_parts]
    res_ref, o_ref = refs[2 * n_parts], refs[2 * n_parts + 1]
    acc = res_ref[...]
    for a_ref, w_ref in zip(a_refs, w_refs):
        acc = acc + _dot(a_ref[...].astype(BF16), w_ref[...])
    o_ref[...] = acc


def _matmul_res(a_parts, w_parts, res, *, tm, tn):
    m, n = res.shape
    assert m % tm == 0 and n % tn == 0
    n_parts = len(a_parts)
    in_specs = ([pl.BlockSpec((tm, a.shape[1]), lambda i, j: (i, 0)) for a in a_parts]
                + [pl.BlockSpec((None, w.shape[0], tn), lambda i, j: (j, 0, 0)) for w in w_parts]
                + [pl.BlockSpec((tm, tn), lambda i, j: (i, j))])
    return pl.pallas_call(
        functools.partial(_matmul_res_kernel, n_parts=n_parts),
        grid=(m // tm, n // tn),
        in_specs=in_specs,
        out_specs=pl.BlockSpec((tm, tn), lambda i, j: (i, j)),
        out_shape=jax.ShapeDtypeStruct((m, n), F32),
        compiler_params=_cparams("parallel", "arbitrary"),
        name="matmul_res",
    )(*a_parts, *[_col_blocks(w, tn) for w in w_parts], res)


def _conv_a_kernel(val_ref, glu_ref, gate_ref, *rest, tl, has_prefix, row_blk, ln_blk, ln_par):
    if has_prefix:
        pre_ref, w_ref, b_ref, lg_ref, lb_ref, ya_ref, st_ref, ubuf, cbuf = rest
    else:
        w_ref, b_ref, lg_ref, lb_ref, ya_ref, st_ref, ubuf, cbuf = rest
    ch = val_ref.shape[1]
    nlb = ch // LANES
    off = CONV_PAD - (CONV_WIDTH - 1)
    lanes = [slice(c * LANES, (c + 1) * LANES) for c in range(nlb)]

    @pl.when(pl.program_id(1) == 0)
    def _():
        for c in range(nlb):
            ubuf[c, 0:CONV_PAD, :] = pre_ref[0, :, lanes[c]] if has_prefix else jnp.zeros((CONV_PAD, LANES), F32)

    for c in range(nlb):
        ubuf[c, CONV_PAD:CONV_PAD + tl, :] = val_ref[:, lanes[c]] * _sigmoid(glu_ref[:, lanes[c]])

    def lane_body(c, carry):
        c0 = pl.multiple_of(c * LANES, LANES)
        taps = [w_ref[k:k + 1, pl.ds(c0, LANES)] for k in range(CONV_WIDTH)]
        bias = b_ref[:, pl.ds(c0, LANES)]
        for r0 in range(0, tl, row_blk):
            acc = jnp.zeros((row_blk, LANES), F32) + bias
            for sh in range(SUBLANES):
                ks = list(range(sh, CONV_WIDTH, SUBLANES))
                span = row_blk + ks[-1] - sh
                win = ubuf[c, r0 + off + sh:r0 + off + sh + span, :]
                for k in ks:
                    acc = acc + win[k - sh:k - sh + row_blk] * taps[k]
            cbuf[c, r0:r0 + row_blk, :] = acc
        return carry

    lax.fori_loop(0, nlb, lane_body, 0)

    for c in range(nlb):
        tail = ubuf[c, tl:tl + CONV_PAD, :]
        st_ref[0, :, lanes[c]] = tail
        ubuf[c, 0:CONV_PAD, :] = tail

    def ln_body(r, carry):
        for j in range(ln_par):
            r0 = pl.multiple_of((r * ln_par + j) * ln_blk, ln_blk)
            u = jnp.concatenate([cbuf[c, pl.ds(r0, ln_blk), :] for c in range(nlb)], axis=1)
            mu = jnp.mean(u, axis=-1, keepdims=True)
            uc = u - mu
            var = jnp.mean(uc * uc, axis=-1, keepdims=True)
            y = uc * lax.rsqrt(var + EPS) * lg_ref[...] + lb_ref[...]
            ya = _silu(y) * _silu(gate_ref[pl.ds(r0, ln_blk), :])
            ya_ref[pl.ds(r0, ln_blk), :] = ya.astype(ya_ref.dtype)
        return carry

    lax.fori_loop(0, tl // (ln_blk * ln_par), ln_body, 0)


def _conv_a(proj, nb, seq, prefix, w, b, lg, lb, *, tl, out_dtype):
    ch = w.shape[1]
    nl = seq // tl
    has_prefix = prefix is not None
    row_blk = min(tl, 64)
    ln_blk = min(tl, 16)
    ln_par = max(1, min(4, tl // ln_blk))
    in_specs = [pl.BlockSpec((tl, ch), lambda bi, i: (bi * nl + i, 0)),
                pl.BlockSpec((tl, ch), lambda bi, i: (bi * nl + i, 1)),
                pl.BlockSpec((tl, ch), lambda bi, i: (bi * nl + i, 2))]
    args = [proj, proj, proj]
    if has_prefix:
        in_specs.append(pl.BlockSpec((1, CONV_PAD, ch), lambda bi, i: (bi, 0, 0)))
        args.append(prefix)
    in_specs += [pl.BlockSpec((CONV_WIDTH, ch), lambda bi, i: (0, 0))] + [pl.BlockSpec((1, ch), lambda bi, i: (0, 0))] * 3
    args += [w, b.reshape(1, ch), lg.reshape(1, ch), lb.reshape(1, ch)]
    ya, st = pl.pallas_call(
        functools.partial(_conv_a_kernel, tl=tl, has_prefix=has_prefix, row_blk=row_blk, ln_blk=ln_blk, ln_par=ln_par),
        grid=(nb, nl),
        in_specs=in_specs,
        out_specs=[pl.BlockSpec((tl, ch), lambda bi, i: (bi * nl + i, 0)),
                   pl.BlockSpec((1, CONV_PAD, ch), lambda bi, i: (bi, 0, 0))],
        out_shape=[jax.ShapeDtypeStruct((nb * seq, ch), out_dtype),
                   jax.ShapeDtypeStruct((nb, CONV_PAD, ch), F32)],
        scratch_shapes=[pltpu.VMEM((ch // LANES, CONV_PAD + tl, LANES), F32), pltpu.VMEM((ch // LANES, tl, LANES), F32)],
        compiler_params=_cparams("parallel", "arbitrary"),
        name="conv_a",
    )(*args)
    return ya, st[:, CONV_PAD - (CONV_WIDTH - 1):, :]


def _ssd_kernel(z_ref, xs_ref, bc_ref, dt_ref, *rest, q, qp, has_state):
    if has_state:
        pre_ref, h0_ref = rest[:2]
        rest = rest[2:]
    (cw_ref, cb_ref, dtb_ref, alog_ref, dsk_ref, ng_ref, e_ref, et_ref,
     yb_ref, cst_ref, hst_ref, xbuf, h_ref) = rest
    inner = xs_ref.shape[1]
    nbc = bc_ref.shape[1]
    xbc_w = inner + nbc
    gw = inner // SSD_GROUPS
    hpg = gw // SSD_HEAD_DIM
    off = SSD_PAD - (SSD_CONV - 1)

    nlb = xbc_w // LANES
    lanes = [slice(c * LANES, (c + 1) * LANES) for c in range(nlb)]

    @pl.when(pl.program_id(1) == 0)
    def _():
        for c in range(nlb):
            xbuf[c, 0:SSD_PAD, :] = pre_ref[0, :, lanes[c]] if has_state else jnp.zeros((SSD_PAD, LANES), F32)
        h_ref[...] = h0_ref[0] if has_state else jnp.zeros(h_ref.shape, F32)

    conv_parts = []
    for c in range(nlb):
        src = xs_ref[:, lanes[c]] if c < inner // LANES else bc_ref[:, c * LANES - inner:(c + 1) * LANES - inner]
        xbuf[c, SSD_PAD:SSD_PAD + q, :] = src
        acc = jnp.zeros((q, LANES), F32) + cb_ref[:, lanes[c]]
        for k in range(SSD_CONV):
            acc = acc + xbuf[c, off + k:off + k + q, :] * cw_ref[k:k + 1, lanes[c]]
        conv_parts.append(_silu(acc))
        tail = xbuf[c, q:q + SSD_PAD, :]
        cst_ref[0, :, lanes[c]] = tail
        xbuf[c, 0:SSD_PAD, :] = tail
    xbc = jnp.concatenate(conv_parts, axis=1)

    dt_raw = dt_ref[...] + dtb_ref[...]
    dt = jnp.maximum(dt_raw, 0.0) + jnp.log(1.0 + jnp.exp(-jnp.abs(dt_raw)))
    zg = z_ref[...]
    if q < qp:
        xbc = jnp.concatenate([xbc, jnp.zeros((qp - q, xbc_w), F32)], axis=0)
        dt = jnp.concatenate([dt, jnp.zeros((qp - q, LANES), F32)], axis=0)
        zg = jnp.concatenate([zg, jnp.zeros((qp - q, inner), F32)], axis=0)
    x = xbc[:, 0:inner]

    a_neg = -jnp.exp(alog_ref[...])
    dta = dt * a_neg
    row = lax.broadcasted_iota(jnp.int32, (qp, qp), 0)
    col = lax.broadcasted_iota(jnp.int32, (qp, qp), 1)
    causal = row >= col
    tril = jnp.where(causal, 1.0, 0.0).astype(BF16)
    acs = sum(_dot(tril, part) for part in _split3(dta))
    acs_t = acs.T
    last = acs[qp - 1:qp, :]
    eat = jnp.exp(acs)
    ddt = jnp.exp(last - acs) * dt

    stack = jnp.concatenate([dt, ddt, eat], axis=0)
    ex = sum(_dot(part, e_ref[...]) for part in _split2(stack))
    xdt = (x * ex[0:qp]).astype(BF16)
    xdd = (x * ex[qp:2 * qp]).astype(BF16)
    eat_x = ex[2 * qp:3 * qp]

    lane = lax.broadcasted_iota(jnp.int32, (qp, LANES), 1)
    lo_half = lane < SSD_HEAD_DIM
    y_parts, s_parts = [], []
    for g in range(SSD_GROUPS):
        bg = xbc[:, inner + g * SSD_STATE:inner + (g + 1) * SSD_STATE].astype(BF16)
        cg = xbc[:, inner + (SSD_GROUPS + g) * SSD_STATE:inner + (SSD_GROUPS + g + 1) * SSD_STATE].astype(BF16)
        cb = _dot_nt(cg, bg)
        hg = h_ref[g * gw:(g + 1) * gw, :].astype(BF16)
        y_off = _dot_nt(cg, hg) * eat_x[:, g * gw:(g + 1) * gw]
        s_parts.append(_dot_tn(xdd[:, g * gw:(g + 1) * gw], bg))
        yd = []
        for j in range(hpg // 2):
            h0 = g * hpg + 2 * j
            ms = []
            for h in (h0, h0 + 1):
                seg = jnp.where(causal, acs[:, h:h + 1] - acs_t[h:h + 1, :], -jnp.inf)
                ms.append(cb * jnp.exp(seg))
            lhs = jnp.concatenate(ms, axis=1).astype(BF16)
            xp = xdt[:, h0 * SSD_HEAD_DIM:h0 * SSD_HEAD_DIM + LANES]
            zero = jnp.zeros_like(xp)
            rhs = jnp.concatenate([jnp.where(lo_half, xp, zero), jnp.where(lo_half, zero, xp)], axis=0)
            yd.append(_dot(lhs, rhs))
        y_parts.append(jnp.concatenate(yd, axis=1) + y_off)
    y = jnp.concatenate(y_parts, axis=1) + dsk_ref[...] * x

    dec = jnp.broadcast_to(jnp.exp(acs_t[:, qp - 1:qp]), (LANES, LANES))
    dec_rows = sum(_dot(et_ref[...], part) for part in _split2(dec))
    h_new = h_ref[...] * dec_rows + jnp.concatenate(s_parts, axis=0)
    h_ref[...] = h_new
    hst_ref[0] = h_new

    y = y * _silu(zg)
    outs = []
    for g in range(SSD_GROUPS):
        yg = y[:, g * gw:(g + 1) * gw]
        outs.append(yg * lax.rsqrt(jnp.mean(yg * yg, axis=-1, keepdims=True) + EPS))
    yn = jnp.concatenate(outs, axis=1) * ng_ref[...]
    yb_ref[...] = yn[0:q].astype(yb_ref.dtype)


def _ssd(proj, dtp, nb, seq, prefix, h0, cw, cb, dt_bias, a_log, d_skip, norm_g, *, col0, q, out_dtype):
    inner = norm_g.shape[0]
    heads = dt_bias.shape[0]
    nbc = 2 * SSD_GROUPS * SSD_STATE
    xbc_w = inner + nbc
    nc = seq // q
    qp = SSD_CHUNK
    has_state = h0 is not None
    zb, xb, bcb = col0 // inner, col0 // inner + 1, (col0 + 2 * inner) // nbc
    assert col0 % inner == 0 and (col0 + 2 * inner) % nbc == 0

    lane_head = jnp.arange(inner, dtype=jnp.int32) // SSD_HEAD_DIM
    expand = (jnp.arange(LANES, dtype=jnp.int32)[:, None] == lane_head[None, :]).astype(BF16)
    pad = LANES - heads
    dtb = jnp.pad(dt_bias.astype(F32), (0, pad)).reshape(1, LANES)
    alog = jnp.pad(a_log.astype(F32), (0, pad)).reshape(1, LANES)
    dsk = jnp.repeat(d_skip.astype(F32), SSD_HEAD_DIM).reshape(1, inner)

    in_specs = [pl.BlockSpec((q, inner), lambda bi, c: (bi * nc + c, zb)),
                pl.BlockSpec((q, inner), lambda bi, c: (bi * nc + c, xb)),
                pl.BlockSpec((q, nbc), lambda bi, c: (bi * nc + c, bcb)),
                pl.BlockSpec((q, LANES), lambda bi, c: (bi * nc + c, 0))]
    args = [proj, proj, proj, dtp]
    if has_state:
        in_specs += [pl.BlockSpec((1, SSD_PAD, xbc_w), lambda bi, c: (bi, 0, 0)),
                     pl.BlockSpec((1, inner, SSD_STATE), lambda bi, c: (bi, 0, 0))]
        args += [prefix, h0]
    const = lambda bi, c: (0, 0)
    in_specs += [pl.BlockSpec((SSD_CONV, xbc_w), const), pl.BlockSpec((1, xbc_w), const),
                 pl.BlockSpec((1, LANES), const), pl.BlockSpec((1, LANES), const),
                 pl.BlockSpec((1, inner), const), pl.BlockSpec((1, inner), const),
                 pl.BlockSpec((LANES, inner), const), pl.BlockSpec((inner, LANES), const)]
    args += [cw, cb.reshape(1, xbc_w), dtb, alog, dsk, norm_g.reshape(1, inner), expand, expand.T]
    yb, cst, hst = pl.pallas_call(
        functools.partial(_ssd_kernel, q=q, qp=qp, has_state=has_state),
        grid=(nb, nc),
        in_specs=in_specs,
        out_specs=[pl.BlockSpec((q, inner), lambda bi, c: (bi * nc + c, 0)),
                   pl.BlockSpec((1, SSD_PAD, xbc_w), lambda bi, c: (bi, 0, 0)),
                   pl.BlockSpec((1, inner, SSD_STATE), lambda bi, c: (bi, 0, 0))],
        out_shape=[jax.ShapeDtypeStruct((nb * seq, inner), out_dtype),
                   jax.ShapeDtypeStruct((nb, SSD_PAD, xbc_w), F32),
                   jax.ShapeDtypeStruct((nb, inner, SSD_STATE), F32)],
        scratch_shapes=[pltpu.VMEM((xbc_w // LANES, SSD_PAD + q, LANES), F32), pltpu.VMEM((inner, SSD_STATE), F32)],
        compiler_params=_cparams("parallel", "arbitrary"),
        name="ssd",
    )(*args)
    return yb, cst[:, SSD_PAD - (SSD_CONV - 1):, :], hst


def _seg_rms(xb, gain, bd):
    ss = sum(_dot(part, bd) for part in _split2(xb * xb))
    return xb * lax.rsqrt(ss * (1.0 / ATT_QK_DIM) + EPS) * gain


Q_SCALE = ATT_QK_DIM ** -0.5 * math.log2(math.e)


def _qk_prep_prompt_kernel(q_ref, k_ref, qg_ref, kg_ref, bd_ref, qo_ref, kt_ref):
    width = q_ref.shape[1]
    bd = bd_ref[...]
    for c in range(width // LANES):
        cs = slice(c * LANES, (c + 1) * LANES)
        kt_ref[0, cs, :] = _seg_rms(k_ref[:, cs], kg_ref[...], bd).T
        qo_ref[:, cs] = (_seg_rms(q_ref[:, cs], qg_ref[...], bd) * Q_SCALE).astype(qo_ref.dtype)


def _qk_prep_sample_kernel(q_ref, k_ref, qg_ref, kg_ref, bd_ref, qo_ref, ko_ref):
    rows, width = q_ref.shape
    bd = bd_ref[...]
    lo_half = lax.broadcasted_iota(jnp.int32, (rows, LANES), 1) < ATT_QK_DIM
    blocks = []
    for c in range(width // LANES):
        cs = slice(c * LANES, (c + 1) * LANES)
        ko_ref[:, cs] = _seg_rms(k_ref[:, cs], kg_ref[...], bd)
        qn = _seg_rms(q_ref[:, cs], qg_ref[...], bd) * Q_SCALE
        blocks += [jnp.where(lo_half, qn, 0.0), jnp.where(lo_half, 0.0, qn)]
    qo_ref[...] = jnp.concatenate(blocks, axis=0).astype(qo_ref.dtype)


def _qk_consts(q_g, k_g):
    seg = jnp.arange(LANES, dtype=jnp.int32) // ATT_QK_DIM
    bd = (seg[:, None] == seg[None, :]).astype(BF16)
    qg = jnp.tile(q_g.astype(F32), LANES // ATT_QK_DIM).reshape(1, LANES)
    kg = jnp.tile(k_g.astype(F32), LANES // ATT_QK_DIM).reshape(1, LANES)
    return qg, kg, bd


def _qk_prep_prompt(proj, q_g, k_g, nb, seq, width, *, tm):
    nl = seq // tm
    const = lambda i: (0, 0)
    return pl.pallas_call(
        _qk_prep_prompt_kernel,
        grid=(nb * nl,),
        in_specs=[pl.BlockSpec((tm, width), lambda i: (i, 0)), pl.BlockSpec((tm, width), lambda i: (i, 1)),
                  pl.BlockSpec((1, LANES), const), pl.BlockSpec((1, LANES), const), pl.BlockSpec((LANES, LANES), const)],
        out_specs=[pl.BlockSpec((tm, width), lambda i: (i, 0)),
                   pl.BlockSpec((1, width, tm), lambda i: (i // nl, 0, i % nl))],
        out_shape=[jax.ShapeDtypeStruct((nb * seq, width), BF16), jax.ShapeDtypeStruct((nb, width, seq), F32)],
        compiler_params=_cparams("parallel"),
        name="qk_prep_prompt",
    )(proj, proj, *_qk_consts(q_g, k_g))


def _qk_prep_sample(proj, q_g, k_g, nb, dec, width):
    q_rows = 2 * (width // LANES) * dec
    const = lambda i: (0, 0)
    return pl.pallas_call(
        _qk_prep_sample_kernel,
        grid=(nb,),
        in_specs=[pl.BlockSpec((dec, width), lambda i: (i, 0)), pl.BlockSpec((dec, width), lambda i: (i, 1)),
                  pl.BlockSpec((1, LANES), const), pl.BlockSpec((1, LANES), const), pl.BlockSpec((LANES, LANES), const)],
        out_specs=[pl.BlockSpec((q_rows, LANES), lambda i: (i, 0)), pl.BlockSpec((dec, width), lambda i: (i, 0))],
        out_shape=[jax.ShapeDtypeStruct((nb * q_rows, LANES), BF16), jax.ShapeDtypeStruct((nb * dec, width), F32)],
        compiler_params=_cparams("parallel"),
        name="qk_prep_sample",
    )(proj, proj, *_qk_consts(q_g, k_g))


def _diff_out(o1, o2, lam, lam_init, sub_g, gate):
    d = o1 - lam * o2
    d = d * lax.rsqrt(jnp.mean(d * d, axis=-1, keepdims=True) + EPS) * sub_g * (1.0 - lam_init)
    return d * _silu(gate)


def _prompt_attn_kernel(lam_ref, q_ref, kt_ref, v_ref, gate_ref, sg_ref, o_ref, ktb_ref, vb_ref, *, tq, lam_init):
    seq = q_ref.shape[0]
    lam = lam_ref[0]
    ktb_ref[...] = kt_ref[0].astype(BF16)
    vb_ref[:, 0:LANES] = v_ref[...].astype(BF16)
    vb_ref[:, LANES:2 * LANES] = jnp.ones((seq, LANES), BF16)
    lane = lax.broadcasted_iota(jnp.int32, (tq, LANES), 1)
    lo_half = lane < ATT_QK_DIM
    row = lax.broadcasted_iota(jnp.int32, (2 * tq, tq), 0)
    col = lax.broadcasted_iota(jnp.int32, (2 * tq, tq), 1)
    diag_ok = col <= jnp.where(row >= tq, row - tq, row)
    def scores(qi):
        lo = qi * tq
        q = q_ref[lo:lo + tq, :]
        zero = jnp.zeros_like(q)
        q2 = jnp.concatenate([jnp.where(lo_half, q, zero), jnp.where(lo_half, zero, q)], axis=0)
        sd = jnp.where(diag_ok, _dot(q2, ktb_ref[:, lo:lo + tq]), NEG)
        sp = _dot(q2, ktb_ref[:, 0:lo]) if qi > 0 else None
        return sd, sp

    def finish(qi, sd, sp):
        lo = qi * tq
        m = jnp.max(sd, axis=-1, keepdims=True)
        if qi > 0:
            m = jnp.maximum(m, jnp.max(sp, axis=-1, keepdims=True))
        acc = _dot(jnp.exp2(sd - m).astype(BF16), vb_ref[lo:lo + tq, :])
        if qi > 0:
            acc = acc + _dot(jnp.exp2(sp - m).astype(BF16), vb_ref[0:lo, :])
        o = acc[:, 0:LANES] / acc[:, LANES:2 * LANES]
        out = _diff_out(o[0:tq], o[tq:2 * tq], lam, lam_init, sg_ref[...], gate_ref[lo:lo + tq, :])
        o_ref[lo:lo + tq, :] = out.astype(o_ref.dtype)

    nq = seq // tq
    cur = scores(0)
    for qi in range(nq):
        nxt = scores(qi + 1) if qi + 1 < nq else None
        finish(qi, *cur)
        cur = nxt


def _prompt_attn(lam, qn, kt, proj, sub_g, nb, seq, heads, *, v_blk, gate_blk, lam_init, tq):
    m = nb * seq
    width = heads * ATT_V_DIM
    blk = lambda off: pl.BlockSpec((seq, LANES), lambda bi, h: (bi, off + h))
    return pl.pallas_call(
        functools.partial(_prompt_attn_kernel, tq=tq, lam_init=lam_init),
        grid=(nb, heads),
        in_specs=[pl.BlockSpec(memory_space=pltpu.SMEM), blk(0),
                  pl.BlockSpec((1, LANES, seq), lambda bi, h: (bi, h, 0)), blk(v_blk), blk(gate_blk),
                  pl.BlockSpec((1, LANES), lambda bi, h: (0, 0))],
        out_specs=blk(0),
        out_shape=jax.ShapeDtypeStruct((m, width), BF16),
        scratch_shapes=[pltpu.VMEM((LANES, seq), BF16), pltpu.VMEM((seq, 2 * LANES), BF16)],
        compiler_params=_cparams("parallel", "parallel"),
        name="prompt_attn",
    )(lam, qn, kt, proj, proj, sub_g.reshape(1, LANES))


def _sample_attn_kernel(pt_ref, lam_ref, q_ref, *rest, heads, dec, lam_init, group):
    kt_refs = rest[:group]
    v_refs = rest[group:2 * group]
    kn_ref, vn_ref, gate_ref, sg_ref, o_ref, m_ref, l_ref, acc_ref = rest[2 * group:]
    p = pl.program_id(1)
    rph = 2 * dec
    rows = heads * rph

    @pl.when(p == 0)
    def _():
        m_ref[...] = jnp.full(m_ref.shape, NEG, F32)
        l_ref[...] = jnp.zeros(l_ref.shape, F32)
        acc_ref[...] = jnp.zeros(acc_ref.shape, F32)

    def q_head(h):
        return q_ref[h * rph:(h + 1) * rph, :]

    def update(s, v_head, mask):
        if mask is not None:
            s = jnp.where(mask, s, NEG)
        m_new = jnp.maximum(m_ref[...], jnp.max(s, axis=-1, keepdims=True))
        alpha = jnp.exp2(m_ref[...] - m_new)
        pr = jnp.exp2(s - m_new)
        l_ref[...] = alpha * l_ref[...] + jnp.sum(pr, axis=-1, keepdims=True)
        prb = pr.astype(BF16)
        pv = [_dot(prb[h * rph:(h + 1) * rph, :], v_head(h)) for h in range(heads)]
        acc_ref[...] = alpha * acc_ref[...] + jnp.concatenate(pv, axis=0)
        m_ref[...] = m_new

    def kt_head(h):
        return jnp.concatenate([r[0, h * LANES:(h + 1) * LANES, :] for r in kt_refs], axis=1).astype(BF16)

    v_pages = [jnp.swapaxes(r[0].reshape(PAGE, heads, ATT_V_DIM), 0, 1) for r in v_refs]

    def v_head(h):
        return jnp.concatenate([vp[h] for vp in v_pages], axis=0).astype(BF16)

    update(jnp.concatenate([_dot(q_head(h), kt_head(h)) for h in range(heads)], axis=0), v_head, None)

    @pl.when(p == pl.num_programs(1) - 1)
    def _():
        zpad = jnp.zeros((PAGE - dec, LANES), F32)

        def new_head(ref, h):
            return jnp.concatenate([ref[:, h * LANES:(h + 1) * LANES], zpad], axis=0).astype(BF16)

        s = jnp.concatenate([_dot_nt(q_head(h), new_head(kn_ref, h)) for h in range(heads)], axis=0)
        row = lax.broadcasted_iota(jnp.int32, (rows, PAGE), 0)
        col = lax.broadcasted_iota(jnp.int32, (rows, PAGE), 1)
        update(s, functools.partial(new_head, vn_ref), col <= row % dec)
        o = acc_ref[...] / l_ref[...]
        lam = lam_ref[0]
        outs = []
        for h in range(heads):
            o1 = o[h * rph:h * rph + dec]
            o2 = o[h * rph + dec:(h + 1) * rph]
            outs.append(_diff_out(o1, o2, lam, lam_init, sg_ref[...], gate_ref[:, h * ATT_V_DIM:(h + 1) * ATT_V_DIM]))
        o_ref[...] = jnp.concatenate(outs, axis=1)


def _sample_attn(page_table, lam, qs, cache_kt, cache_v, kn, proj, sub_g, nb, dec, heads, *, page0, v_blk, gate_blk,
                 lam_init):
    width = heads * ATT_V_DIM
    rows = heads * 2 * dec
    n_pages = page_table.shape[1]
    group = math.gcd(n_pages, PAGE_GROUP)

    def page(i):
        return lambda bi, p, pt: (page0 + pt[bi, group * p + i], 0, 0)

    grid_spec = pltpu.PrefetchScalarGridSpec(
        num_scalar_prefetch=1,
        grid=(nb, n_pages // group),
        in_specs=([pl.BlockSpec(memory_space=pltpu.SMEM),
                   pl.BlockSpec((rows, LANES), lambda bi, p, pt: (bi, 0))]
                  + [pl.BlockSpec((1, width, PAGE), page(i)) for i in range(group)]
                  + [pl.BlockSpec((1, PAGE * heads, LANES), page(i)) for i in range(group)]
                  + [pl.BlockSpec((dec, width), lambda bi, p, pt: (bi, 0)),
                     pl.BlockSpec((dec, width), lambda bi, p, pt: (bi, v_blk)),
                     pl.BlockSpec((dec, width), lambda bi, p, pt: (bi, gate_blk)),
                     pl.BlockSpec((1, LANES), lambda bi, p, pt: (0, 0))]),
        out_specs=pl.BlockSpec((dec, width), lambda bi, p, pt: (bi, 0)),
        scratch_shapes=[pltpu.VMEM((rows, 1), F32), pltpu.VMEM((rows, 1), F32), pltpu.VMEM((rows, ATT_V_DIM), F32)],
    )
    return pl.pallas_call(
        functools.partial(_sample_attn_kernel, heads=heads, dec=dec, lam_init=lam_init, group=group),
        grid_spec=grid_spec,
        out_shape=jax.ShapeDtypeStruct((nb * dec, width), F32),
        compiler_params=_cparams("parallel", "arbitrary"),
        name="sample_attn",
    )(page_table, lam, qs, *([cache_kt] * group), *([cache_v] * group), kn, proj, proj, sub_g.reshape(1, LANES))


def _even_layer(x, nb, seq, conv_prev, ssd_conv_prev, ssd_prev, p, *, tm, tl, q, act_dtype):
    ch = p["conv_w"].shape[1]
    proj, dtp = _norm_matmul(x, p["norm_g"], p["w_main"], p["w_dt"], tm=tm, tn=1024)
    if conv_prev is not None:
        conv_prev = jnp.pad(conv_prev, ((0, 0), (CONV_PAD - (CONV_WIDTH - 1), 0), (0, 0)))
        ssd_conv_prev = jnp.pad(ssd_conv_prev, ((0, 0), (SSD_PAD - (SSD_CONV - 1), 0), (0, 0)))
        ssd_prev = ssd_prev.reshape(nb, -1, SSD_STATE)
    ya, conv_new = _conv_a(proj, nb, seq, conv_prev, p["conv_w"], p["conv_b"], p["ln_g"], p["ln_b"],
                           tl=tl, out_dtype=act_dtype)
    yb, ssd_conv_new, ssd_new = _ssd(proj, dtp, nb, seq, ssd_conv_prev, ssd_prev, p["ssd_conv_w"], p["ssd_conv_b"],
                                     p["dt_bias"], p["a_log"], p["d_skip"], p["ssd_norm_g"],
                                     col0=3 * ch, q=q, out_dtype=act_dtype)
    y = _matmul_res([ya, yb], [p["w_out_a"], p["w_out_b"]], x, tm=tm, tn=512)
    return y, conv_new, ssd_conv_new, ssd_new


def kernel(x_prompt, x_sample, state_conv_a, state_ssd_conv, state_ssd, cache_k, cache_v, page_table, norm_even, w_in_even, conv_a_w, conv_a_b, conv_a_ln_g, conv_a_ln_b, ssd_conv_w, ssd_conv_b, ssd_dt_bias, ssd_a_log, ssd_d, ssd_norm_g, w_out_even, norm_odd, w_in_odd, q_norm_g, k_norm_g, lam_q1, lam_k1, lam_q2, lam_k2, subln_g, w_out_odd):
    bp, seq, d = x_prompt.shape
    bs, dec, _ = x_sample.shape
    depth = norm_even.shape[0] + norm_odd.shape[0]
    xp = x_prompt.reshape(bp * seq, d)
    xs = x_sample.reshape(bs * dec, d)
    ch = conv_a_w.shape[2]
    ssd_heads = ssd_dt_bias.shape[1]
    inner = ssd_norm_g.shape[1]
    att_width = w_out_odd.shape[1]
    heads = att_width // ATT_V_DIM
    main_cols = w_in_even.shape[2] - ssd_heads

    tm_p = min(1024, bp * seq)
    tq = min(256, seq)

    ca_p, ca_s, sc_p, sc_s, ss_p, ss_s = [], [], [], [], [], []
    k_p, k_s, v_p, v_s = [], [], [], []
    for layer in range(depth):
        j = layer // 2
        if layer % 2 == 0:
            w_in = w_in_even[j]
            p = dict(norm_g=norm_even[j],
                     w_main=w_in[:, :main_cols].astype(BF16),
                     w_dt=jnp.pad(w_in[:, main_cols:], ((0, 0), (0, LANES - ssd_heads))).astype(BF16),
                     conv_w=conv_a_w[j], conv_b=conv_a_b[j], ln_g=conv_a_ln_g[j], ln_b=conv_a_ln_b[j],
                     ssd_conv_w=ssd_conv_w[j], ssd_conv_b=ssd_conv_b[j], dt_bias=ssd_dt_bias[j], a_log=ssd_a_log[j],
                     d_skip=ssd_d[j], ssd_norm_g=ssd_norm_g[j],
                     w_out_a=w_out_even[j][:ch].astype(BF16), w_out_b=w_out_even[j][ch:].astype(BF16))
            xp, c1, c2, c3 = _even_layer(xp, bp, seq, None, None, None, p, tm=tm_p, tl=256, q=SSD_CHUNK, act_dtype=BF16)
            xs, d1, d2, d3 = _even_layer(xs, bs, dec, state_conv_a[j], state_ssd_conv[j], state_ssd[j], p,
                                         tm=bs * dec, tl=dec, q=dec, act_dtype=F32)
            ca_p.append(c1)
            sc_p.append(c2)
            ss_p.append(c3.reshape(bp, ssd_heads, SSD_HEAD_DIM, SSD_STATE))
            ca_s.append(d1)
            sc_s.append(d2)
            ss_s.append(d3.reshape(bs, ssd_heads, SSD_HEAD_DIM, SSD_STATE))
        else:
            lam_init = 0.8 - 0.6 * math.exp(-0.3 * layer)
            lam = (jnp.exp(jnp.sum(lam_q1[j].astype(F32) * lam_k1[j].astype(F32)))
                   - jnp.exp(jnp.sum(lam_q2[j].astype(F32) * lam_k2[j].astype(F32))) + lam_init).reshape(1)
            w_in = w_in_odd[j].astype(BF16)
            w_out = w_out_odd[j].astype(BF16)
            v_blk, gate_blk = 2 * att_width // LANES, 3 * att_width // LANES

            proj = _norm_matmul(xp, norm_odd[j], w_in, tm=tm_p, tn=1024)
            qn, kt = _qk_prep_prompt(proj, q_norm_g[j], k_norm_g[j], bp, seq, att_width, tm=min(512, seq))
            o = _prompt_attn(lam, qn, kt, proj, subln_g[j], bp, seq, heads,
                             v_blk=v_blk, gate_blk=gate_blk, lam_init=lam_init, tq=tq)
            k_p.append(jnp.transpose(kt.reshape(bp, heads, 2, ATT_QK_DIM, seq), (0, 4, 1, 2, 3)))
            v_p.append(proj[:, 2 * att_width:3 * att_width].reshape(bp, seq, heads, ATT_V_DIM))
            xp = _matmul_res([o], [w_out], xp, tm=tm_p, tn=512)

            proj = _norm_matmul(xs, norm_odd[j], w_in, tm=bs * dec, tn=1024)
            qs, kn = _qk_prep_sample(proj, q_norm_g[j], k_norm_g[j], bs, dec, att_width)
            n_pool = cache_k.shape[1]
            cache_kt = jnp.transpose(cache_k, (0, 1, 3, 4, 5, 2)).reshape(-1, att_width, PAGE)
            o = _sample_attn(page_table, lam, qs, cache_kt, cache_v.reshape(-1, PAGE * heads, ATT_V_DIM), kn, proj,
                             subln_g[j], bs, dec, heads, page0=j * n_pool, v_blk=2, gate_blk=3, lam_init=lam_init)
            k_s.append(kn.reshape(bs, dec, heads, 2, ATT_QK_DIM))
            v_s.append(proj[:, 2 * att_width:3 * att_width].reshape(bs, dec, heads, ATT_V_DIM))
            xs = _matmul_res([o], [w_out], xs, tm=bs * dec, tn=512)
    return (xp.reshape(bp, seq, d), xs.reshape(bs, dec, d), jnp.stack(ca_p), jnp.stack(ca_s), jnp.stack(sc_p),
            jnp.stack(sc_s), jnp.stack(ss_p), jnp.stack(ss_s), jnp.stack(k_p), jnp.stack(k_s), jnp.stack(v_p),
            jnp.stack(v_s))
```

```python
import functools
import math

import jax
import jax.numpy as jnp
from jax import lax
from jax.experimental import pallas as pl
from jax.experimental.pallas import tpu as pltpu

F32 = jnp.float32
BF16 = jnp.bfloat16

EPS = 1e-6
NEG = -1e30
LANES = 128
SUBLANES = 8
CONV_WIDTH = 31
CONV_PAD = 32
SSD_HEAD_DIM = 64
SSD_GROUPS = 4
SSD_STATE = 128
SSD_CONV = 4
SSD_PAD = 8
SSD_CHUNK = 128
ATT_QK_DIM = 64
ATT_V_DIM = 128
PAGE = 128
PAGE_GROUP = 8
VMEM_LIMIT = 56 * 1024 * 1024


def _cparams(*sem):
    return pltpu.CompilerParams(dimension_semantics=sem, vmem_limit_bytes=VMEM_LIMIT)


def _sigmoid(x):
    return 0.5 * jnp.tanh(0.5 * x) + 0.5


def _silu(x):
    return x * _sigmoid(x)


def _split2(x):
    hi = x.astype(BF16)
    lo = (x - hi.astype(F32)).astype(BF16)
    return hi, lo


def _split3(x):
    hi = x.astype(BF16)
    r = x - hi.astype(F32)
    mid = r.astype(BF16)
    lo = (r - mid.astype(F32)).astype(BF16)
    return hi, mid, lo


def _dot(a, b):
    return jnp.dot(a, b, preferred_element_type=F32)


def _dot_nt(a, b):
    return lax.dot_general(a, b, (((1,), (1,)), ((), ())), preferred_element_type=F32)


def _dot_tn(a, b):
    return lax.dot_general(a, b, (((0,), (0,)), ((), ())), preferred_element_type=F32)


def _norm_matmul_kernel(x_ref, g_ref, w_ref, *rest, has_aux):
    if has_aux:
        wa_ref, o_ref, oa_ref, hn_ref = rest
    else:
        o_ref, hn_ref = rest

    @pl.when(pl.program_id(1) == 0)
    def _():
        x = x_ref[...]
        ms = jnp.mean(x * x, axis=-1, keepdims=True)
        hn_ref[...] = (x * lax.rsqrt(ms + EPS) * g_ref[...]).astype(BF16)
        if has_aux:
            oa_ref[...] = _dot(hn_ref[...], wa_ref[...])

    o_ref[...] = _dot(hn_ref[...], w_ref[...])


def _norm_matmul(x, g, w, w_aux=None, *, n, tm, tn):
    m, k = x.shape
    assert m % tm == 0 and n % tn == 0 and n <= w.shape[1]
    has_aux = w_aux is not None
    in_specs = [pl.BlockSpec((tm, k), lambda i, j: (i, 0)),
                pl.BlockSpec((1, k), lambda i, j: (0, 0)),
                pl.BlockSpec((k, tn), lambda i, j: (0, j))]
    out_specs = [pl.BlockSpec((tm, tn), lambda i, j: (i, j))]
    out_shape = [jax.ShapeDtypeStruct((m, n), F32)]
    args = [x, g.reshape(1, k), w]
    if has_aux:
        na = w_aux.shape[1]
        in_specs.append(pl.BlockSpec((k, na), lambda i, j: (0, 0)))
        out_specs.append(pl.BlockSpec((tm, na), lambda i, j: (i, 0)))
        out_shape.append(jax.ShapeDtypeStruct((m, na), F32))
        args.append(w_aux)
    outs = pl.pallas_call(
        functools.partial(_norm_matmul_kernel, has_aux=has_aux),
        grid=(m // tm, n // tn),
        in_specs=in_specs, out_specs=out_specs, out_shape=out_shape,
        scratch_shapes=[pltpu.VMEM((tm, k), BF16)],
        compiler_params=_cparams("parallel", "arbitrary"),
        name="norm_matmul",
    )(*args)
    return outs if has_aux else outs[0]


def _matmul_res_kernel(*refs, n_parts):
    a_refs = refs[:n_parts]
    w_refs = refs[n_parts:2 * n_parts]
    res_ref, o_ref = refs[2 * n_parts], refs[2 * n_parts + 1]
    acc = res_ref[...]
    for a_ref, w_ref in zip(a_refs, w_refs):
        acc = acc + _dot(a_ref[...].astype(BF16), w_ref[...])
    o_ref[...] = acc


def _matmul_res(a_parts, w, res, *, tm, tn):
    m, n = res.shape
    assert m % tm == 0 and n % tn == 0
    n_parts = len(a_parts)
    kp = w.shape[0] // n_parts
    assert all(a.shape[1] == kp for a in a_parts)
    in_specs = ([pl.BlockSpec((tm, kp), lambda i, j: (i, 0)) for _ in a_parts]
                + [pl.BlockSpec((kp, tn), lambda i, j, p=p: (p, j)) for p in range(n_parts)]
                + [pl.BlockSpec((tm, tn), lambda i, j: (i, j))])
    return pl.pallas_call(
        functools.partial(_matmul_res_kernel, n_parts=n_parts),
        grid=(m // tm, n // tn),
        in_specs=in_specs,
        out_specs=pl.BlockSpec((tm, tn), lambda i, j: (i, j)),
        out_shape=jax.ShapeDtypeStruct((m, n), F32),
        compiler_params=_cparams("parallel", "arbitrary"),
        name="matmul_res",
    )(*a_parts, *([w] * n_parts), res)


def _conv_a_kernel(val_ref, glu_ref, gate_ref, *rest, tl, has_prefix, row_blk, ln_blk, ln_par):
    if has_prefix:
        pre_ref, w_ref, b_ref, lg_ref, lb_ref, ya_ref, st_ref, ubuf, cbuf = rest
    else:
        w_ref, b_ref, lg_ref, lb_ref, ya_ref, st_ref, ubuf, cbuf = rest
    ch = val_ref.shape[1]
    nlb = ch // LANES
    off = CONV_PAD - (CONV_WIDTH - 1)
    lanes = [slice(c * LANES, (c + 1) * LANES) for c in range(nlb)]

    @pl.when(pl.program_id(1) == 0)
    def _():
        for c in range(nlb):
            ubuf[c, 0:CONV_PAD, :] = pre_ref[0, :, lanes[c]] if has_prefix else jnp.zeros((CONV_PAD, LANES), F32)

    for c in range(nlb):
        ubuf[c, CONV_PAD:CONV_PAD + tl, :] = val_ref[:, lanes[c]] * _sigmoid(glu_ref[:, lanes[c]])

    def lane_body(c, carry):
        c0 = pl.multiple_of(c * LANES, LANES)
        taps = [w_ref[k:k + 1, pl.ds(c0, LANES)] for k in range(CONV_WIDTH)]
        bias = b_ref[:, pl.ds(c0, LANES)]
        for r0 in range(0, tl, row_blk):
            acc = jnp.zeros((row_blk, LANES), F32) + bias
            for sh in range(SUBLANES):
                ks = list(range(sh, CONV_WIDTH, SUBLANES))
                span = row_blk + ks[-1] - sh
                win = ubuf[c, r0 + off + sh:r0 + off + sh + span, :]
                for k in ks:
                    acc = acc + win[k - sh:k - sh + row_blk] * taps[k]
            cbuf[c, r0:r0 + row_blk, :] = acc
        return carry

    lax.fori_loop(0, nlb, lane_body, 0)

    for c in range(nlb):
        tail = ubuf[c, tl:tl + CONV_PAD, :]
        st_ref[0, :, lanes[c]] = tail
        ubuf[c, 0:CONV_PAD, :] = tail

    def ln_body(r, carry):
        for j in range(ln_par):
            r0 = pl.multiple_of((r * ln_par + j) * ln_blk, ln_blk)
            u = jnp.concatenate([cbuf[c, pl.ds(r0, ln_blk), :] for c in range(nlb)], axis=1)
            mu = jnp.mean(u, axis=-1, keepdims=True)
            uc = u - mu
            var = jnp.mean(uc * uc, axis=-1, keepdims=True)
            y = uc * lax.rsqrt(var + EPS) * lg_ref[...] + lb_ref[...]
            ya = _silu(y) * _silu(gate_ref[pl.ds(r0, ln_blk), :])
            ya_ref[pl.ds(r0, ln_blk), :] = ya.astype(ya_ref.dtype)
        return carry

    lax.fori_loop(0, tl // (ln_blk * ln_par), ln_body, 0)


def _conv_a(proj, nb, seq, prefix, w, b, lg, lb, *, tl, out_dtype):
    ch = w.shape[1]
    nl = seq // tl
    has_prefix = prefix is not None
    row_blk = min(tl, 64)
    ln_blk = min(tl, 16)
    ln_par = max(1, min(4, tl // ln_blk))
    in_specs = [pl.BlockSpec((tl, ch), lambda bi, i: (bi * nl + i, 0)),
                pl.BlockSpec((tl, ch), lambda bi, i: (bi * nl + i, 1)),
                pl.BlockSpec((tl, ch), lambda bi, i: (bi * nl + i, 2))]
    args = [proj, proj, proj]
    if has_prefix:
        in_specs.append(pl.BlockSpec((1, CONV_PAD, ch), lambda bi, i: (bi, 0, 0)))
        args.append(prefix)
    in_specs += [pl.BlockSpec((CONV_WIDTH, ch), lambda bi, i: (0, 0))] + [pl.BlockSpec((1, ch), lambda bi, i: (0, 0))] * 3
    args += [w, b.reshape(1, ch), lg.reshape(1, ch), lb.reshape(1, ch)]
    ya, st = pl.pallas_call(
        functools.partial(_conv_a_kernel, tl=tl, has_prefix=has_prefix, row_blk=row_blk, ln_blk=ln_blk, ln_par=ln_par),
        grid=(nb, nl),
        in_specs=in_specs,
        out_specs=[pl.BlockSpec((tl, ch), lambda bi, i: (bi * nl + i, 0)),
                   pl.BlockSpec((1, CONV_PAD, ch), lambda bi, i: (bi, 0, 0))],
        out_shape=[jax.ShapeDtypeStruct((nb * seq, ch), out_dtype),
                   jax.ShapeDtypeStruct((nb, CONV_PAD, ch), F32)],
        scratch_shapes=[pltpu.VMEM((ch // LANES, CONV_PAD + tl, LANES), F32), pltpu.VMEM((ch // LANES, tl, LANES), F32)],
        compiler_params=_cparams("parallel", "arbitrary"),
        name="conv_a",
    )(*args)
    return ya, st[:, CONV_PAD - (CONV_WIDTH - 1):, :]


def _ssd_kernel(z_ref, xs_ref, bc_ref, dt_ref, *rest, q, qp, has_state):
    if has_state:
        pre_ref, h0_ref = rest[:2]
        rest = rest[2:]
    (cw_ref, cb_ref, dtb_ref, alog_ref, dsk_ref, ng_ref, e_ref, et_ref,
     yb_ref, cst_ref, hst_ref, xbuf, h_ref) = rest
    inner = xs_ref.shape[1]
    nbc = bc_ref.shape[1]
    xbc_w = inner + nbc
    gw = inner // SSD_GROUPS
    hpg = gw // SSD_HEAD_DIM
    off = SSD_PAD - (SSD_CONV - 1)

    nlb = xbc_w // LANES
    lanes = [slice(c * LANES, (c + 1) * LANES) for c in range(nlb)]

    @pl.when(pl.program_id(1) == 0)
    def _():
        for c in range(nlb):
            xbuf[c, 0:SSD_PAD, :] = pre_ref[0, :, lanes[c]] if has_state else jnp.zeros((SSD_PAD, LANES), F32)
        h_ref[...] = h0_ref[0] if has_state else jnp.zeros(h_ref.shape, F32)

    conv_parts = []
    for c in range(nlb):
        src = xs_ref[:, lanes[c]] if c < inner // LANES else bc_ref[:, c * LANES - inner:(c + 1) * LANES - inner]
        xbuf[c, SSD_PAD:SSD_PAD + q, :] = src
        acc = jnp.zeros((q, LANES), F32) + cb_ref[:, lanes[c]]
        for k in range(SSD_CONV):
            acc = acc + xbuf[c, off + k:off + k + q, :] * cw_ref[k:k + 1, lanes[c]]
        conv_parts.append(_silu(acc))
        tail = xbuf[c, q:q + SSD_PAD, :]
        cst_ref[0, :, lanes[c]] = tail
        xbuf[c, 0:SSD_PAD, :] = tail
    xbc = jnp.concatenate(conv_parts, axis=1)

    dt_raw = dt_ref[...] + dtb_ref[...]
    dt = jnp.maximum(dt_raw, 0.0) + jnp.log(1.0 + jnp.exp(-jnp.abs(dt_raw)))
    zg = z_ref[...]
    if q < qp:
        xbc = jnp.concatenate([xbc, jnp.zeros((qp - q, xbc_w), F32)], axis=0)
        dt = jnp.concatenate([dt, jnp.zeros((qp - q, LANES), F32)], axis=0)
        zg = jnp.concatenate([zg, jnp.zeros((qp - q, inner), F32)], axis=0)
    x = xbc[:, 0:inner]

    a_neg = -jnp.exp(alog_ref[...])
    dta = dt * a_neg
    row = lax.broadcasted_iota(jnp.int32, (qp, qp), 0)
    col = lax.broadcasted_iota(jnp.int32, (qp, qp), 1)
    causal = row >= col
    tril = jnp.where(causal, 1.0, 0.0).astype(BF16)
    acs = sum(_dot(tril, part) for part in _split3(dta))
    acs_t = acs.T
    last = acs[qp - 1:qp, :]
    eat = jnp.exp(acs)
    ddt = jnp.exp(last - acs) * dt

    stack = jnp.concatenate([dt, ddt, eat], axis=0)
    ex = sum(_dot(part, e_ref[...]) for part in _split2(stack))
    xdt = (x * ex[0:qp]).astype(BF16)
    xdd = (x * ex[qp:2 * qp]).astype(BF16)
    eat_x = ex[2 * qp:3 * qp]

    lane = lax.broadcasted_iota(jnp.int32, (qp, LANES), 1)
    lo_half = lane < SSD_HEAD_DIM
    y_parts, s_parts = [], []
    for g in range(SSD_GROUPS):
        bg = xbc[:, inner + g * SSD_STATE:inner + (g + 1) * SSD_STATE].astype(BF16)
        cg = xbc[:, inner + (SSD_GROUPS + g) * SSD_STATE:inner + (SSD_GROUPS + g + 1) * SSD_STATE].astype(BF16)
        cb = _dot_nt(cg, bg)
        hg = h_ref[g * gw:(g + 1) * gw, :].astype(BF16)
        y_off = _dot_nt(cg, hg) * eat_x[:, g * gw:(g + 1) * gw]
        s_parts.append(_dot_tn(xdd[:, g * gw:(g + 1) * gw], bg))
        yd = []
        for j in range(hpg // 2):
            h0 = g * hpg + 2 * j
            ms = []
            for h in (h0, h0 + 1):
                seg = jnp.where(causal, acs[:, h:h + 1] - acs_t[h:h + 1, :], -jnp.inf)
                ms.append(cb * jnp.exp(seg))
            lhs = jnp.concatenate(ms, axis=1).astype(BF16)
            xp = xdt[:, h0 * SSD_HEAD_DIM:h0 * SSD_HEAD_DIM + LANES]
            zero = jnp.zeros_like(xp)
            rhs = jnp.concatenate([jnp.where(lo_half, xp, zero), jnp.where(lo_half, zero, xp)], axis=0)
            yd.append(_dot(lhs, rhs))
        y_parts.append(jnp.concatenate(yd, axis=1) + y_off)
    y = jnp.concatenate(y_parts, axis=1) + dsk_ref[...] * x

    dec = jnp.broadcast_to(jnp.exp(acs_t[:, qp - 1:qp]), (LANES, LANES))
    dec_rows = sum(_dot(et_ref[...], part) for part in _split2(dec))
    h_new = h_ref[...] * dec_rows + jnp.concatenate(s_parts, axis=0)
    h_ref[...] = h_new
    hst_ref[0] = h_new

    y = y * _silu(zg)
    outs = []
    for g in range(SSD_GROUPS):
        yg = y[:, g * gw:(g + 1) * gw]
        outs.append(yg * lax.rsqrt(jnp.mean(yg * yg, axis=-1, keepdims=True) + EPS))
    yn = jnp.concatenate(outs, axis=1) * ng_ref[...]
    yb_ref[...] = yn[0:q].astype(yb_ref.dtype)


def _ssd(proj, dtp, nb, seq, prefix, h0, cw, cb, dt_bias, a_log, d_skip, norm_g, *, col0, q, out_dtype):
    inner = norm_g.shape[0]
    heads = dt_bias.shape[0]
    nbc = 2 * SSD_GROUPS * SSD_STATE
    xbc_w = inner + nbc
    nc = seq // q
    qp = SSD_CHUNK
    has_state = h0 is not None
    zb, xb, bcb = col0 // inner, col0 // inner + 1, (col0 + 2 * inner) // nbc
    assert col0 % inner == 0 and (col0 + 2 * inner) % nbc == 0

    lane_head = jnp.arange(inner, dtype=jnp.int32) // SSD_HEAD_DIM
    expand = (jnp.arange(LANES, dtype=jnp.int32)[:, None] == lane_head[None, :]).astype(BF16)
    pad = LANES - heads
    dtb = jnp.pad(dt_bias.astype(F32), (0, pad)).reshape(1, LANES)
    alog = jnp.pad(a_log.astype(F32), (0, pad)).reshape(1, LANES)
    dsk = jnp.repeat(d_skip.astype(F32), SSD_HEAD_DIM).reshape(1, inner)

    in_specs = [pl.BlockSpec((q, inner), lambda bi, c: (bi * nc + c, zb)),
                pl.BlockSpec((q, inner), lambda bi, c: (bi * nc + c, xb)),
                pl.BlockSpec((q, nbc), lambda bi, c: (bi * nc + c, bcb)),
                pl.BlockSpec((q, LANES), lambda bi, c: (bi * nc + c, 0))]
    args = [proj, proj, proj, dtp]
    if has_state:
        in_specs += [pl.BlockSpec((1, SSD_PAD, xbc_w), lambda bi, c: (bi, 0, 0)),
                     pl.BlockSpec((1, inner, SSD_STATE), lambda bi, c: (bi, 0, 0))]
        args += [prefix, h0]
    const = lambda bi, c: (0, 0)
    in_specs += [pl.BlockSpec((SSD_CONV, xbc_w), const), pl.BlockSpec((1, xbc_w), const),
                 pl.BlockSpec((1, LANES), const), pl.BlockSpec((1, LANES), const),
                 pl.BlockSpec((1, inner), const), pl.BlockSpec((1, inner), const),
                 pl.BlockSpec((LANES, inner), const), pl.BlockSpec((inner, LANES), const)]
    args += [cw, cb.reshape(1, xbc_w), dtb, alog, dsk, norm_g.reshape(1, inner), expand, expand.T]
    yb, cst, hst = pl.pallas_call(
        functools.partial(_ssd_kernel, q=q, qp=qp, has_state=has_state),
        grid=(nb, nc),
        in_specs=in_specs,
        out_specs=[pl.BlockSpec((q, inner), lambda bi, c: (bi * nc + c, 0)),
                   pl.BlockSpec((1, SSD_PAD, xbc_w), lambda bi, c: (bi, 0, 0)),
                   pl.BlockSpec((1, inner, SSD_STATE), lambda bi, c: (bi, 0, 0))],
        out_shape=[jax.ShapeDtypeStruct((nb * seq, inner), out_dtype),
                   jax.ShapeDtypeStruct((nb, SSD_PAD, xbc_w), F32),
                   jax.ShapeDtypeStruct((nb, inner, SSD_STATE), F32)],
        scratch_shapes=[pltpu.VMEM((xbc_w // LANES, SSD_PAD + q, LANES), F32), pltpu.VMEM((inner, SSD_STATE), F32)],
        compiler_params=_cparams("parallel", "arbitrary"),
        name="ssd",
    )(*args)
    return yb, cst[:, SSD_PAD - (SSD_CONV - 1):, :], hst


def _seg_rms(xb, gain, bd):
    ss = sum(_dot(part, bd) for part in _split2(xb * xb))
    return xb * lax.rsqrt(ss * (1.0 / ATT_QK_DIM) + EPS) * gain


Q_SCALE = ATT_QK_DIM ** -0.5 * math.log2(math.e)


def _qk_prep_prompt_kernel(q_ref, k_ref, qg_ref, kg_ref, bd_ref, qo_ref, kt_ref):
    width = q_ref.shape[1]
    bd = bd_ref[...]
    for c in range(width // LANES):
        cs = slice(c * LANES, (c + 1) * LANES)
        kt_ref[0, cs, :] = _seg_rms(k_ref[:, cs], kg_ref[...], bd).T
        qo_ref[:, cs] = (_seg_rms(q_ref[:, cs], qg_ref[...], bd) * Q_SCALE).astype(qo_ref.dtype)


def _qk_prep_sample_kernel(q_ref, k_ref, qg_ref, kg_ref, bd_ref, qo_ref, ko_ref):
    rows, width = q_ref.shape
    bd = bd_ref[...]
    lo_half = lax.broadcasted_iota(jnp.int32, (rows, LANES), 1) < ATT_QK_DIM
    blocks = []
    for c in range(width // LANES):
        cs = slice(c * LANES, (c + 1) * LANES)
        ko_ref[:, cs] = _seg_rms(k_ref[:, cs], kg_ref[...], bd)
        qn = _seg_rms(q_ref[:, cs], qg_ref[...], bd) * Q_SCALE
        blocks += [jnp.where(lo_half, qn, 0.0), jnp.where(lo_half, 0.0, qn)]
    qo_ref[...] = jnp.concatenate(blocks, axis=0).astype(qo_ref.dtype)


def _qk_consts(q_g, k_g):
    seg = jnp.arange(LANES, dtype=jnp.int32) // ATT_QK_DIM
    bd = (seg[:, None] == seg[None, :]).astype(BF16)
    qg = jnp.tile(q_g.astype(F32), LANES // ATT_QK_DIM).reshape(1, LANES)
    kg = jnp.tile(k_g.astype(F32), LANES // ATT_QK_DIM).reshape(1, LANES)
    return qg, kg, bd


def _qk_prep_prompt(proj, q_g, k_g, nb, seq, width, *, tm):
    nl = seq // tm
    const = lambda i: (0, 0)
    return pl.pallas_call(
        _qk_prep_prompt_kernel,
        grid=(nb * nl,),
        in_specs=[pl.BlockSpec((tm, width), lambda i: (i, 0)), pl.BlockSpec((tm, width), lambda i: (i, 1)),
                  pl.BlockSpec((1, LANES), const), pl.BlockSpec((1, LANES), const), pl.BlockSpec((LANES, LANES), const)],
        out_specs=[pl.BlockSpec((tm, width), lambda i: (i, 0)),
                   pl.BlockSpec((1, width, tm), lambda i: (i // nl, 0, i % nl))],
        out_shape=[jax.ShapeDtypeStruct((nb * seq, width), BF16), jax.ShapeDtypeStruct((nb, width, seq), F32)],
        compiler_params=_cparams("parallel"),
        name="qk_prep_prompt",
    )(proj, proj, *_qk_consts(q_g, k_g))


def _qk_prep_sample(proj, q_g, k_g, nb, dec, width):
    q_rows = 2 * (width // LANES) * dec
    const = lambda i: (0, 0)
    return pl.pallas_call(
        _qk_prep_sample_kernel,
        grid=(nb,),
        in_specs=[pl.BlockSpec((dec, width), lambda i: (i, 0)), pl.BlockSpec((dec, width), lambda i: (i, 1)),
                  pl.BlockSpec((1, LANES), const), pl.BlockSpec((1, LANES), const), pl.BlockSpec((LANES, LANES), const)],
        out_specs=[pl.BlockSpec((q_rows, LANES), lambda i: (i, 0)), pl.BlockSpec((dec, width), lambda i: (i, 0))],
        out_shape=[jax.ShapeDtypeStruct((nb * q_rows, LANES), BF16), jax.ShapeDtypeStruct((nb * dec, width), F32)],
        compiler_params=_cparams("parallel"),
        name="qk_prep_sample",
    )(proj, proj, *_qk_consts(q_g, k_g))


def _diff_out(o1, o2, lam, lam_init, sub_g, gate):
    d = o1 - lam * o2
    d = d * lax.rsqrt(jnp.mean(d * d, axis=-1, keepdims=True) + EPS) * sub_g * (1.0 - lam_init)
    return d * _silu(gate)


def _prompt_attn_kernel(lam_ref, q_ref, kt_ref, v_ref, gate_ref, sg_ref, o_ref, ktb_ref, vb_ref, *, tq, lam_init):
    seq = q_ref.shape[0]
    lam = lam_ref[0]
    ktb_ref[...] = kt_ref[0].astype(BF16)
    vb_ref[:, 0:LANES] = v_ref[...].astype(BF16)
    vb_ref[:, LANES:2 * LANES] = jnp.ones((seq, LANES), BF16)
    lane = lax.broadcasted_iota(jnp.int32, (tq, LANES), 1)
    lo_half = lane < ATT_QK_DIM
    row = lax.broadcasted_iota(jnp.int32, (2 * tq, tq), 0)
    col = lax.broadcasted_iota(jnp.int32, (2 * tq, tq), 1)
    diag_ok = col <= jnp.where(row >= tq, row - tq, row)
    def scores(qi):
        lo = qi * tq
        q = q_ref[lo:lo + tq, :]
        zero = jnp.zeros_like(q)
        q2 = jnp.concatenate([jnp.where(lo_half, q, zero), jnp.where(lo_half, zero, q)], axis=0)
        sd = jnp.where(diag_ok, _dot(q2, ktb_ref[:, lo:lo + tq]), NEG)
        sp = _dot(q2, ktb_ref[:, 0:lo]) if qi > 0 else None
        return sd, sp

    def finish(qi, sd, sp):
        lo = qi * tq
        m = jnp.max(sd, axis=-1, keepdims=True)
        if qi > 0:
            m = jnp.maximum(m, jnp.max(sp, axis=-1, keepdims=True))
        acc = _dot(jnp.exp2(sd - m).astype(BF16), vb_ref[lo:lo + tq, :])
        if qi > 0:
            acc = acc + _dot(jnp.exp2(sp - m).astype(BF16), vb_ref[0:lo, :])
        o = acc[:, 0:LANES] / acc[:, LANES:2 * LANES]
        out = _diff_out(o[0:tq], o[tq:2 * tq], lam, lam_init, sg_ref[...], gate_ref[lo:lo + tq, :])
        o_ref[lo:lo + tq, :] = out.astype(o_ref.dtype)

    nq = seq // tq
    cur = scores(0)
    for qi in range(nq):
        nxt = scores(qi + 1) if qi + 1 < nq else None
        finish(qi, *cur)
        cur = nxt


def _prompt_attn(lam, qn, kt, proj, sub_g, nb, seq, heads, *, v_blk, gate_blk, lam_init, tq):
    m = nb * seq
    width = heads * ATT_V_DIM
    blk = lambda off: pl.BlockSpec((seq, LANES), lambda bi, h: (bi, off + h))
    return pl.pallas_call(
        functools.partial(_prompt_attn_kernel, tq=tq, lam_init=lam_init),
        grid=(nb, heads),
        in_specs=[pl.BlockSpec(memory_space=pltpu.SMEM), blk(0),
                  pl.BlockSpec((1, LANES, seq), lambda bi, h: (bi, h, 0)), blk(v_blk), blk(gate_blk),
                  pl.BlockSpec((1, LANES), lambda bi, h: (0, 0))],
        out_specs=blk(0),
        out_shape=jax.ShapeDtypeStruct((m, width), BF16),
        scratch_shapes=[pltpu.VMEM((LANES, seq), BF16), pltpu.VMEM((seq, 2 * LANES), BF16)],
        compiler_params=_cparams("parallel", "parallel"),
        name="prompt_attn",
    )(lam, qn, kt, proj, proj, sub_g.reshape(1, LANES))


def _sample_attn_kernel(pt_ref, lam_ref, q_ref, *rest, heads, dec, lam_init, group):
    kt_refs = rest[:group]
    v_refs = rest[group:2 * group]
    kn_ref, vn_ref, gate_ref, sg_ref, o_ref, m_ref, l_ref, acc_ref = rest[2 * group:]
    p = pl.program_id(1)
    rph = 2 * dec
    rows = heads * rph

    @pl.when(p == 0)
    def _():
        m_ref[...] = jnp.full(m_ref.shape, NEG, F32)
        l_ref[...] = jnp.zeros(l_ref.shape, F32)
        acc_ref[...] = jnp.zeros(acc_ref.shape, F32)

    def q_head(h):
        return q_ref[h * rph:(h + 1) * rph, :]

    def update(s, v_head, mask):
        if mask is not None:
            s = jnp.where(mask, s, NEG)
        m_new = jnp.maximum(m_ref[...], jnp.max(s, axis=-1, keepdims=True))
        alpha = jnp.exp2(m_ref[...] - m_new)
        pr = jnp.exp2(s - m_new)
        l_ref[...] = alpha * l_ref[...] + jnp.sum(pr, axis=-1, keepdims=True)
        prb = pr.astype(BF16)
        pv = [_dot(prb[h * rph:(h + 1) * rph, :], v_head(h)) for h in range(heads)]
        acc_ref[...] = alpha * acc_ref[...] + jnp.concatenate(pv, axis=0)
        m_ref[...] = m_new

    def kt_head(h):
        return jnp.concatenate([r[0, h * LANES:(h + 1) * LANES, :] for r in kt_refs], axis=1).astype(BF16)

    v_pages = [jnp.swapaxes(r[0].reshape(PAGE, heads, ATT_V_DIM), 0, 1) for r in v_refs]

    def v_head(h):
        return jnp.concatenate([vp[h] for vp in v_pages], axis=0).astype(BF16)

    update(jnp.concatenate([_dot(q_head(h), kt_head(h)) for h in range(heads)], axis=0), v_head, None)

    @pl.when(p == pl.num_programs(1) - 1)
    def _():
        zpad = jnp.zeros((PAGE - dec, LANES), F32)

        def new_head(ref, h):
            return jnp.concatenate([ref[:, h * LANES:(h + 1) * LANES], zpad], axis=0).astype(BF16)

        s = jnp.concatenate([_dot_nt(q_head(h), new_head(kn_ref, h)) for h in range(heads)], axis=0)
        row = lax.broadcasted_iota(jnp.int32, (rows, PAGE), 0)
        col = lax.broadcasted_iota(jnp.int32, (rows, PAGE), 1)
        update(s, functools.partial(new_head, vn_ref), col <= row % dec)
        o = acc_ref[...] / l_ref[...]
        lam = lam_ref[0]
        outs = []
        for h in range(heads):
            o1 = o[h * rph:h * rph + dec]
            o2 = o[h * rph + dec:(h + 1) * rph]
            outs.append(_diff_out(o1, o2, lam, lam_init, sg_ref[...], gate_ref[:, h * ATT_V_DIM:(h + 1) * ATT_V_DIM]))
        o_ref[...] = jnp.concatenate(outs, axis=1)


def _sample_attn(page_table, lam, qs, cache_kt, cache_v, kn, proj, sub_g, nb, dec, heads, *, page0, v_blk, gate_blk,
                 lam_init):
    width = heads * ATT_V_DIM
    rows = heads * 2 * dec
    n_pages = page_table.shape[1]
    group = math.gcd(n_pages, PAGE_GROUP)

    def page(i):
        return lambda bi, p, pt: (page0 + pt[bi, group * p + i], 0, 0)

    grid_spec = pltpu.PrefetchScalarGridSpec(
        num_scalar_prefetch=1,
        grid=(nb, n_pages // group),
        in_specs=([pl.BlockSpec(memory_space=pltpu.SMEM),
                   pl.BlockSpec((rows, LANES), lambda bi, p, pt: (bi, 0))]
                  + [pl.BlockSpec((1, width, PAGE), page(i)) for i in range(group)]
                  + [pl.BlockSpec((1, PAGE * heads, LANES), page(i)) for i in range(group)]
                  + [pl.BlockSpec((dec, width), lambda bi, p, pt: (bi, 0)),
                     pl.BlockSpec((dec, width), lambda bi, p, pt: (bi, v_blk)),
                     pl.BlockSpec((dec, width), lambda bi, p, pt: (bi, gate_blk)),
                     pl.BlockSpec((1, LANES), lambda bi, p, pt: (0, 0))]),
        out_specs=pl.BlockSpec((dec, width), lambda bi, p, pt: (bi, 0)),
        scratch_shapes=[pltpu.VMEM((rows, 1), F32), pltpu.VMEM((rows, 1), F32), pltpu.VMEM((rows, ATT_V_DIM), F32)],
    )
    return pl.pallas_call(
        functools.partial(_sample_attn_kernel, heads=heads, dec=dec, lam_init=lam_init, group=group),
        grid_spec=grid_spec,
        out_shape=jax.ShapeDtypeStruct((nb * dec, width), F32),
        compiler_params=_cparams("parallel", "arbitrary"),
        name="sample_attn",
    )(page_table, lam, qs, *([cache_kt] * group), *([cache_v] * group), kn, proj, proj, sub_g.reshape(1, LANES))


def _even_layer(x, nb, seq, conv_prev, ssd_conv_prev, ssd_prev, p, *, tm, tl, q, act_dtype):
    ch = p["conv_w"].shape[1]
    proj, dtp = _norm_matmul(x, p["norm_g"], p["w_in"], p["w_dt"], n=p["main_cols"], tm=tm, tn=1024)
    if conv_prev is not None:
        conv_prev = jnp.pad(conv_prev, ((0, 0), (CONV_PAD - (CONV_WIDTH - 1), 0), (0, 0)))
        ssd_conv_prev = jnp.pad(ssd_conv_prev, ((0, 0), (SSD_PAD - (SSD_CONV - 1), 0), (0, 0)))
        ssd_prev = ssd_prev.reshape(nb, -1, SSD_STATE)
    ya, conv_new = _conv_a(proj, nb, seq, conv_prev, p["conv_w"], p["conv_b"], p["ln_g"], p["ln_b"],
                           tl=tl, out_dtype=act_dtype)
    yb, ssd_conv_new, ssd_new = _ssd(proj, dtp, nb, seq, ssd_conv_prev, ssd_prev, p["ssd_conv_w"], p["ssd_conv_b"],
                                     p["dt_bias"], p["a_log"], p["d_skip"], p["ssd_norm_g"],
                                     col0=3 * ch, q=q, out_dtype=act_dtype)
    y = _matmul_res([ya, yb], p["w_out"], x, tm=tm, tn=512)
    return y, conv_new, ssd_conv_new, ssd_new


def kernel(x_prompt, x_sample, state_conv_a, state_ssd_conv, state_ssd, cache_k, cache_v, page_table, norm_even, w_in_even, conv_a_w, conv_a_b, conv_a_ln_g, conv_a_ln_b, ssd_conv_w, ssd_conv_b, ssd_dt_bias, ssd_a_log, ssd_d, ssd_norm_g, w_out_even, norm_odd, w_in_odd, q_norm_g, k_norm_g, lam_q1, lam_k1, lam_q2, lam_k2, subln_g, w_out_odd):
    bp, seq, d = x_prompt.shape
    bs, dec, _ = x_sample.shape
    depth = norm_even.shape[0] + norm_odd.shape[0]
    xp = x_prompt.reshape(bp * seq, d)
    xs = x_sample.reshape(bs * dec, d)
    ch = conv_a_w.shape[2]
    ssd_heads = ssd_dt_bias.shape[1]
    inner = ssd_norm_g.shape[1]
    att_width = w_out_odd.shape[1]
    heads = att_width // ATT_V_DIM
    main_cols = w_in_even.shape[2] - ssd_heads

    tm_p = min(1024, bp * seq)
    tq = min(256, seq)

    ca_p, ca_s, sc_p, sc_s, ss_p, ss_s = [], [], [], [], [], []
    k_p, k_s, v_p, v_s = [], [], [], []
    for layer in range(depth):
        j = layer // 2
        if layer % 2 == 0:
            w_in = w_in_even[j]
            p = dict(norm_g=norm_even[j],
                     w_in=w_in.astype(BF16), main_cols=main_cols,
                     w_dt=jnp.pad(w_in[:, main_cols:], ((0, 0), (0, LANES - ssd_heads))).astype(BF16),
                     conv_w=conv_a_w[j], conv_b=conv_a_b[j], ln_g=conv_a_ln_g[j], ln_b=conv_a_ln_b[j],
                     ssd_conv_w=ssd_conv_w[j], ssd_conv_b=ssd_conv_b[j], dt_bias=ssd_dt_bias[j], a_log=ssd_a_log[j],
                     d_skip=ssd_d[j], ssd_norm_g=ssd_norm_g[j],
                     w_out=w_out_even[j].astype(BF16))
            xp, c1, c2, c3 = _even_layer(xp, bp, seq, None, None, None, p, tm=tm_p, tl=256, q=SSD_CHUNK, act_dtype=BF16)
            xs, d1, d2, d3 = _even_layer(xs, bs, dec, state_conv_a[j], state_ssd_conv[j], state_ssd[j], p,
                                         tm=bs * dec, tl=dec, q=dec, act_dtype=F32)
            ca_p.append(c1)
            sc_p.append(c2)
            ss_p.append(c3.reshape(bp, ssd_heads, SSD_HEAD_DIM, SSD_STATE))
            ca_s.append(d1)
            sc_s.append(d2)
            ss_s.append(d3.reshape(bs, ssd_heads, SSD_HEAD_DIM, SSD_STATE))
        else:
            lam_init = 0.8 - 0.6 * math.exp(-0.3 * layer)
            lam = (jnp.exp(jnp.sum(lam_q1[j].astype(F32) * lam_k1[j].astype(F32)))
                   - jnp.exp(jnp.sum(lam_q2[j].astype(F32) * lam_k2[j].astype(F32))) + lam_init).reshape(1)
            w_in = w_in_odd[j].astype(BF16)
            w_out = w_out_odd[j].astype(BF16)
            v_blk, gate_blk = 2 * att_width // LANES, 3 * att_width // LANES

            proj = _norm_matmul(xp, norm_odd[j], w_in, n=w_in.shape[1], tm=tm_p, tn=1024)
            qn, kt = _qk_prep_prompt(proj, q_norm_g[j], k_norm_g[j], bp, seq, att_width, tm=min(512, seq))
            o = _prompt_attn(lam, qn, kt, proj, subln_g[j], bp, seq, heads,
                             v_blk=v_blk, gate_blk=gate_blk, lam_init=lam_init, tq=tq)
            k_p.append(jnp.transpose(kt.reshape(bp, heads, 2, ATT_QK_DIM, seq), (0, 4, 1, 2, 3)))
            v_p.append(proj[:, 2 * att_width:3 * att_width].reshape(bp, seq, heads, ATT_V_DIM))
            xp = _matmul_res([o], w_out, xp, tm=tm_p, tn=512)

            proj = _norm_matmul(xs, norm_odd[j], w_in, n=w_in.shape[1], tm=bs * dec, tn=1024)
            qs, kn = _qk_prep_sample(proj, q_norm_g[j], k_norm_g[j], bs, dec, att_width)
            n_pool = cache_k.shape[1]
            cache_kt = jnp.transpose(cache_k, (0, 1, 3, 4, 5, 2)).reshape(-1, att_width, PAGE)
            o = _sample_attn(page_table, lam, qs, cache_kt, cache_v.reshape(-1, PAGE * heads, ATT_V_DIM), kn, proj,
                             subln_g[j], bs, dec, heads, page0=j * n_pool, v_blk=2, gate_blk=3, lam_init=lam_init)
            k_s.append(kn.reshape(bs, dec, heads, 2, ATT_QK_DIM))
            v_s.append(proj[:, 2 * att_width:3 * att_width].reshape(bs, dec, heads, ATT_V_DIM))
            xs = _matmul_res([o], w_out, xs, tm=bs * dec, tn=512)
    return (xp.reshape(bp, seq, d), xs.reshape(bs, dec, d), jnp.stack(ca_p), jnp.stack(ca_s), jnp.stack(sc_p),
            jnp.stack(sc_s), jnp.stack(ss_p), jnp.stack(ss_s), jnp.stack(k_p), jnp.stack(k_s), jnp.stack(v_p),
            jnp.stack(v_s))
```
